```python
import math
import jax, jax.numpy as jnp
from jax import lax
import numpy as np

D_MODEL = 1024
BATCH = 8
SEQ = 2048
DEPTH = 2

CHUNK = 64
Q_BLOCK = 128
A_Q_BLOCK = 64
EPS = 1e-6

A_HEADS = 8
A_HEAD_DIM = 64
IDX_HEADS = 8
IDX_DIM = 64
TOPK_MAX = 256

B_HEADS = 8
Q_LORA = 256
KV_LORA = 128
QK_NOPE = 64
QK_ROPE = 32
V_DIM = 64
ROPE_THETA = 10000.0

N_BUCKETS = 32
MAX_DISTANCE = 128

N_EXPERTS = 32
TOP_K = 4
D_EXPERT = 1024
SWIGLU_LIMIT = 7.0
SWIGLU_ALPHA = 1.702
MOE_BLOCK = 256

D_A = A_HEADS * A_HEAD_DIM
D_B = B_HEADS * V_DIM
D_MIX = D_A + D_B
IN_SIZES = (D_A, D_A, D_A, IDX_HEADS * IDX_DIM, IDX_DIM, IDX_HEADS, Q_LORA, KV_LORA, QK_ROPE)
D_IN = 3 * D_A + IDX_HEADS * IDX_DIM + IDX_DIM + IDX_HEADS + Q_LORA + KV_LORA + QK_ROPE

kernel_name = 'hybrid_dsa_mla_moe_adaln_trunk'


def rmsnorm(x, g):
    xf = x.astype(jnp.float32)
    y = xf * lax.rsqrt(jnp.mean(xf * xf, axis=-1, keepdims=True) + EPS)
    return (y * g).astype(x.dtype)


def rope(x, pos):
    half = x.shape[-1] // 2
    inv = ROPE_THETA ** (-jnp.arange(half, dtype=jnp.float32) / half)
    ang = pos.astype(jnp.float32)[..., None] * inv
    ang = ang.reshape(ang.shape[:2] + (1,) * (x.ndim - 3) + (half,))
    cos, sin = jnp.cos(ang), jnp.sin(ang)
    xf = x.astype(jnp.float32)
    x1, x2 = xf[..., :half], xf[..., half:]
    return jnp.concatenate([x1 * cos - x2 * sin, x2 * cos + x1 * sin], axis=-1).astype(x.dtype)


def t5_bucket(rel):
    nb = N_BUCKETS // 2
    max_exact = nb // 2
    ret = jnp.where(rel > 0, nb, 0)
    n = jnp.abs(rel)
    large = max_exact + (jnp.log(jnp.maximum(n, 1).astype(jnp.float32) / max_exact)
                         / math.log(MAX_DISTANCE / max_exact) * (nb - max_exact)).astype(jnp.int32)
    large = jnp.minimum(large, nb - 1)
    return ret + jnp.where(n < max_exact, n, large)


def to_blocks(t, qb):
    b, s = t.shape[:2]
    return jnp.moveaxis(t.reshape((b, s // qb, qb) + t.shape[2:]), 1, 0)


def from_blocks(t):
    nqb, b, qb = t.shape[:3]
    return jnp.moveaxis(t, 0, 1).reshape((b, nqb * qb) + t.shape[3:])


def indexer_sparse_attention(q_a, k_a, v_a, q_idx, k_idx, w_idx, pos, rel_bias):
    b, s = pos.shape
    topk = min(TOPK_MAX, s // 4)
    kv = jnp.concatenate([k_a, v_a], axis=-1)
    key_chunk = pos // CHUNK

    def block(args):
        q, qi, wi, pq = args
        rel_scores = jax.nn.relu(jnp.einsum('bqhd,bsd->bqhs', qi, k_idx))
        score = jnp.einsum('bqh,bqhs->bqs', wi, rel_scores).astype(jnp.float32)
        visible = key_chunk[:, None, :] <= (pq // CHUNK)[:, :, None]
        top_val, top_idx = lax.top_k(jnp.where(visible, score, -jnp.inf), topk)
        valid = jnp.isfinite(top_val)
        kv_sel = jax.vmap(lambda t, i: t[i])(kv, top_idx)
        k_sel, v_sel = jnp.split(kv_sel, 2, axis=-1)
        pos_sel = jax.vmap(lambda p, i: p[i])(pos, top_idx)
        bias = rel_bias[t5_bucket(pos_sel - pq[:, :, None])]
        logits = jnp.einsum('bqhd,bqkhd->bqkh', q, k_sel).astype(jnp.float32) * (A_HEAD_DIM ** -0.5) + bias
        p = jax.nn.softmax(jnp.where(valid[..., None], logits, -jnp.inf), axis=2)
        return jnp.einsum('bqkh,bqkhd->bqhd', p.astype(v_sel.dtype), v_sel)

    out = lax.map(block, (to_blocks(q_a, A_Q_BLOCK), to_blocks(q_idx, A_Q_BLOCK),
                          to_blocks(w_idx, A_Q_BLOCK), to_blocks(pos, A_Q_BLOCK)))
    return from_blocks(out)


def latent_attention(q_nope, q_rope, k_nope, k_rope, v, pos):
    scale = (QK_NOPE + QK_ROPE) ** -0.5
    key_chunk = pos // CHUNK

    def block(args):
        qn, qr, pq = args
        logits = (jnp.einsum('bqhd,bshd->bhqs', qn, k_nope)
                  + jnp.einsum('bqhd,bsd->bhqs', qr, k_rope)).astype(jnp.float32) * scale
        visible = key_chunk[:, None, None, :] <= (pq // CHUNK)[:, None, :, None]
        p = jax.nn.softmax(jnp.where(visible, logits, -jnp.inf), axis=-1)
        return jnp.einsum('bhqs,bshd->bqhd', p.astype(v.dtype), v)

    out = lax.map(block, (to_blocks(q_nope, Q_BLOCK), to_blocks(q_rope, Q_BLOCK), to_blocks(pos, Q_BLOCK)))
    return from_blocks(out)


def hybrid_mixer(h, pos, rel_bias, w_in, q_norm, w_uq, kv_norm, w_ukv, w_out):
    b, s, _ = h.shape
    offsets = np.cumsum(IN_SIZES)[:-1].tolist()
    proj = h @ w_in
    q_a, k_a, v_a, q_idx, k_idx, w_idx, c_q, c_kv, k_rope = jnp.split(proj, offsets, axis=-1)
    heads = lambda t, n: t.reshape(b, s, n, -1)
    y_a = indexer_sparse_attention(heads(q_a, A_HEADS), heads(k_a, A_HEADS), heads(v_a, A_HEADS),
                                   heads(q_idx, IDX_HEADS), k_idx, w_idx, pos, rel_bias)
    q = heads(rmsnorm(c_q, q_norm) @ w_uq, B_HEADS)
    kvb = heads(rmsnorm(c_kv, kv_norm) @ w_ukv, B_HEADS)
    y_b = latent_attention(q[..., :QK_NOPE], rope(q[..., QK_NOPE:], pos),
                           kvb[..., :QK_NOPE], rope(k_rope, pos), kvb[..., QK_NOPE:], pos)
    y = jnp.concatenate([y_a.reshape(b, s, D_A), y_b.reshape(b, s, D_B)], axis=-1)
    return y @ w_out


def clamped_swiglu(hid):
    x_glu, x_lin = jnp.split(hid, 2, axis=-1)
    x_glu = jnp.minimum(x_glu, SWIGLU_LIMIT)
    x_lin = jnp.clip(x_lin, -SWIGLU_LIMIT, SWIGLU_LIMIT)
    return x_glu * jax.nn.sigmoid(SWIGLU_ALPHA * x_glu) * (x_lin + 1.0)


def routed_moe(h, w_router, b_router, w1, b1, w2, b2):
    b, s, d = h.shape
    n = b * s
    xt = h.reshape(n, d)
    logits = (xt @ w_router + b_router).astype(jnp.float32)
    top_val, top_e = lax.top_k(logits, TOP_K)
    gate = jax.nn.softmax(top_val, axis=-1)
    nk = n * TOP_K
    flat_e = top_e.reshape(nk)
    flat_tok = jnp.repeat(jnp.arange(n, dtype=jnp.int32), TOP_K)
    flat_w = gate.reshape(nk)
    order = jnp.argsort(flat_e)
    se, stok, sw = flat_e[order], flat_tok[order], flat_w[order]
    counts = jnp.bincount(flat_e, length=N_EXPERTS)
    start = jnp.cumsum(counts) - counts
    padded = (counts + MOE_BLOCK - 1) // MOE_BLOCK * MOE_BLOCK
    pend = jnp.cumsum(padded)
    pstart = pend - padded
    dest = pstart[se] + (jnp.arange(nk) - start[se])
    n_blocks = -(-nk // MOE_BLOCK) + N_EXPERTS
    cap = n_blocks * MOE_BLOCK
    buf_tok = jnp.zeros((cap,), jnp.int32).at[dest].set(stok)
    buf_w = jnp.zeros((cap,), jnp.float32).at[dest].set(sw)
    block_e = jnp.minimum(jnp.searchsorted(pend, jnp.arange(n_blocks) * MOE_BLOCK, side='right'),
                          N_EXPERTS - 1)

    def expert_block(args):
        e, tok, w = args
        xe = xt[tok]
        y = clamped_swiglu(xe @ w1[e] + b1[e]) @ w2[e] + b2[e]
        return y * w[:, None]

    yb = lax.map(expert_block, (block_e, buf_tok.reshape(n_blocks, MOE_BLOCK), buf_w.reshape(n_blocks, MOE_BLOCK)))
    out = jax.ops.segment_sum(yb.reshape(cap, d), buf_tok, num_segments=n)
    return out.reshape(b, s, d).astype(h.dtype)


def setup_inputs(seed: int = 0) -> dict:
    key = jax.random.key(seed)
    ks = jax.random.split(key, 24)
    f32 = jnp.float32

    def nrm(k, shape, fan_in, mult=1.0):
        return jax.random.normal(k, shape, f32) * (mult * fan_in ** -0.5)

    x = jax.random.normal(ks[0], (BATCH, SEQ, D_MODEL), f32)
    c = jax.random.normal(ks[1], (BATCH, D_MODEL), f32)
    offset = jax.random.randint(ks[2], (BATCH, 1), 0, 64) * CHUNK
    positions = (offset + jnp.arange(SEQ, dtype=jnp.int32)[None, :]).astype(jnp.int32)
    return {
        'x': x,
        'c': c,
        'positions': positions,
        'rel_bias': 0.2 * jax.random.normal(ks[3], (N_BUCKETS, A_HEADS), f32),
        'norm_mix': 1.0 + 0.02 * jax.random.normal(ks[4], (DEPTH, D_MODEL), f32),
        'w_ada': nrm(ks[5], (DEPTH, D_MODEL, 6 * D_MODEL), D_MODEL, 0.5),
        'b_ada': 0.02 * jax.random.normal(ks[6], (DEPTH, 6 * D_MODEL), f32),
        'w_in': nrm(ks[7], (DEPTH, D_MODEL, D_IN), D_MODEL),
        'q_norm': 1.0 + 0.02 * jax.random.normal(ks[8], (DEPTH, Q_LORA), f32),
        'w_uq': nrm(ks[9], (DEPTH, Q_LORA, B_HEADS * (QK_NOPE + QK_ROPE)), Q_LORA),
        'kv_norm': 1.0 + 0.02 * jax.random.normal(ks[10], (DEPTH, KV_LORA), f32),
        'w_ukv': nrm(ks[11], (DEPTH, KV_LORA, B_HEADS * (QK_NOPE + V_DIM)), KV_LORA),
        'w_out': nrm(ks[12], (DEPTH, D_MIX, D_MODEL), D_MIX),
        'norm_ffn': 1.0 + 0.02 * jax.random.normal(ks[13], (DEPTH, D_MODEL), f32),
        'w_router': nrm(ks[14], (DEPTH, D_MODEL, N_EXPERTS), D_MODEL),
        'b_router': 0.01 * jax.random.normal(ks[15], (DEPTH, N_EXPERTS), f32),
        'w1': nrm(ks[16], (DEPTH, N_EXPERTS, D_MODEL, 2 * D_EXPERT), D_MODEL),
        'b1': 0.02 * jax.random.normal(ks[17], (DEPTH, N_EXPERTS, 2 * D_EXPERT), f32),
        'w2': nrm(ks[18], (DEPTH, N_EXPERTS, D_EXPERT, D_MODEL), D_EXPERT),
        'b2': 0.02 * jax.random.normal(ks[19], (DEPTH, N_EXPERTS, D_MODEL), f32),
        'norm_final': 1.0 + 0.02 * jax.random.normal(ks[20], (D_MODEL,), f32),
    }


def reference(x, c, positions, rel_bias, norm_mix, w_ada, b_ada, w_in, q_norm, w_uq, kv_norm, w_ukv,
              w_out, norm_ffn, w_router, b_router, w1, b1, w2, b2, norm_final):
    cond = jax.nn.silu(c)
    for l in range(DEPTH):
        mod = (cond @ w_ada[l] + b_ada[l])[:, None, :]
        sh_m, sc_m, g_m, sh_f, sc_f, g_f = jnp.split(mod, 6, axis=-1)
        h = rmsnorm(x, norm_mix[l]) * (1.0 + sc_m) + sh_m
        x = x + g_m * hybrid_mixer(h, positions, rel_bias, w_in[l], q_norm[l], w_uq[l],
                                   kv_norm[l], w_ukv[l], w_out[l])
        h = rmsnorm(x, norm_ffn[l]) * (1.0 + sc_f) + sh_f
        x = x + g_f * routed_moe(h, w_router[l], b_router[l], w1[l], b1[l], w2[l], b2[l])
    return rmsnorm(x, norm_final)
```

```python
import functools
import math

import numpy as np
import jax
import jax.numpy as jnp
from jax import lax
from jax.experimental import pallas as pl
from jax.experimental.pallas import tpu as pltpu

CHUNK = 64
EPS = 1e-6
A_HEADS = 8
A_HEAD_DIM = 64
IDX_HEADS = 8
IDX_DIM = 64
TOPK_MAX = 256
B_HEADS = 8
Q_LORA = 256
KV_LORA = 128
QK_NOPE = 64
QK_ROPE = 32
V_DIM = 64
ROPE_THETA = 10000.0
N_BUCKETS = 32
MAX_DISTANCE = 128
TOP_K = 4
SWIGLU_LIMIT = 7.0
SWIGLU_ALPHA = 1.702
MOE_BLOCK = 256

LANES = 128
VMEM_LIMIT = 56 * 1024 * 1024
INT_MIN = -2 ** 31
NEG_INF = float("-inf")

F32 = jnp.float32
BF16 = jnp.bfloat16
I32 = jnp.int32

_T5_EXACT = (N_BUCKETS // 2) // 2
_T5_THRESH = tuple(
    int(math.ceil(_T5_EXACT * (MAX_DISTANCE / _T5_EXACT) ** (k / (N_BUCKETS // 2 - _T5_EXACT)) - 1e-9))
    for k in range(1, N_BUCKETS // 2 - _T5_EXACT))
_T5_FAR = _T5_THRESH[-1]


def _cparams(sem, vmem=VMEM_LIMIT):
    return pltpu.CompilerParams(dimension_semantics=sem, vmem_limit_bytes=vmem)


def _dot_t(a, b):
    return lax.dot_general(a, b, (((1,), (1,)), ((), ())), preferred_element_type=F32)


def _ada_kernel(c_ref, w_ref, b_ref, o_ref):
    c = c_ref[...]
    cond = c * jax.nn.sigmoid(c)
    o_ref[0] = jnp.dot(cond, w_ref[0], preferred_element_type=F32,
                       precision=lax.Precision.HIGHEST) + b_ref[0]


def _ada_mod(c, w_ada, b_ada, tn=512):
    depth, d, n6 = w_ada.shape
    b = c.shape[0]
    return pl.pallas_call(
        _ada_kernel,
        grid=(depth, n6 // tn),
        in_specs=[pl.BlockSpec((b, d), lambda l, j: (0, 0)),
                  pl.BlockSpec((1, d, tn), lambda l, j: (l, 0, j)),
                  pl.BlockSpec((1, 1, tn), lambda l, j: (l, 0, j))],
        out_specs=pl.BlockSpec((1, b, tn), lambda l, j: (l, 0, j)),
        out_shape=jax.ShapeDtypeStruct((depth, b, n6), F32),
        compiler_params=_cparams(("arbitrary", "arbitrary")),
        name="ada_mod",
    )(c, w_ada, b_ada.reshape(depth, 1, n6))


D_A = A_HEADS * A_HEAD_DIM
D_B = B_HEADS * V_DIM
HB = 128
_C_QA, _C_KA, _C_VA, _C_QI = 0, D_A, 2 * D_A, 3 * D_A
_C_KI = 4 * D_A
_C_WI = _C_KI + LANES
_C_CQ = _C_WI + LANES
_C_CKV = _C_CQ + Q_LORA
_C_KR = _C_CKV + KV_LORA
_C_KRS = _C_KR + LANES
_C_END = _C_KRS + LANES


def _pack_in_weights(w_in):
    d = w_in.shape[0]
    offs = np.cumsum((0, D_A, D_A, D_A, IDX_HEADS * IDX_DIM, IDX_DIM, IDX_HEADS, Q_LORA, KV_LORA, QK_ROPE))
    seg = [w_in[:, offs[i]:offs[i + 1]] for i in range(9)]
    q_a, k_a, v_a, q_i, k_i, w_i, c_q, c_kv, k_r = seg
    z = lambda n: jnp.zeros((d, n), w_in.dtype)
    half = QK_ROPE // 2
    k_rs = jnp.concatenate([k_r[:, half:], k_r[:, :half]], axis=1)
    cols = [q_a * (A_HEAD_DIM ** -0.5), k_a, v_a, q_i, k_i, k_i, w_i, z(LANES - IDX_HEADS), c_q, c_kv,
            z(QK_NOPE), k_r, z(HB - QK_NOPE - QK_ROPE), z(QK_NOPE), k_rs, z(HB - QK_NOPE - QK_ROPE)]
    return jnp.concatenate(cols, axis=1).astype(BF16)


def _pack_uq(w_uq):
    r = w_uq.shape[0]
    w = w_uq.reshape(r, B_HEADS, QK_NOPE + QK_ROPE)
    nope, rope = w[..., :QK_NOPE], w[..., QK_NOPE:]
    half = QK_ROPE // 2
    z = jnp.zeros((r, B_HEADS, HB - QK_NOPE - QK_ROPE), w.dtype)
    main = jnp.concatenate([nope, rope, z], axis=-1).reshape(r, B_HEADS * HB)
    swap = jnp.concatenate([jnp.zeros_like(nope), rope[..., half:], rope[..., :half], z], axis=-1)
    return main.astype(BF16), swap.reshape(r, B_HEADS * HB).astype(BF16)


def _pack_ukv(w_ukv):
    r = w_ukv.shape[0]
    w = w_ukv.reshape(r, B_HEADS, QK_NOPE + V_DIM)
    k = jnp.concatenate([w[..., :QK_NOPE], jnp.zeros((r, B_HEADS, HB - QK_NOPE), w.dtype)], axis=-1)
    v = w[..., QK_NOPE:]
    return k.reshape(r, B_HEADS * HB).astype(BF16), v.reshape(r, B_HEADS * V_DIM).astype(BF16)


def _rope_tables(positions):
    half = QK_ROPE // 2
    inv = ROPE_THETA ** (-jnp.arange(half, dtype=F32) / half)
    ang = positions.astype(F32).reshape(-1, 1) * inv[None, :]
    cos, sin = jnp.cos(ang), jnp.sin(ang)
    n = ang.shape[0]
    one = jnp.ones((n, QK_NOPE), F32)
    z = lambda k: jnp.zeros((n, k), F32)
    ctab = jnp.concatenate([one, cos, cos, z(HB - QK_NOPE - QK_ROPE)], axis=1)
    stab = jnp.concatenate([z(QK_NOPE), -sin, sin, z(HB - QK_NOPE - QK_ROPE)], axis=1)
    return ctab, stab


def _rms(x, g):
    return x * lax.rsqrt(jnp.mean(x * x, axis=-1, keepdims=True) + EPS) * g


def _proj_kernel(x_ref, g_ref, sc_ref, sh_ref, wp_ref, qn_ref, kvn_ref, wuq_ref, wuqs_ref, wukk_ref, wukv_ref,
                 ct_ref, st_ref,
                 qa_ref, ka_ref, va_ref, qi_ref, ki_ref, wi_ref, qb_ref, kb_ref, vb_ref):
    x = x_ref[...]
    h = _rms(x, g_ref[...]) * (1.0 + sc_ref[0]) + sh_ref[0]
    hb = h.astype(BF16)
    seg = lambda a, b: jnp.dot(hb, wp_ref[:, a:b], preferred_element_type=F32)
    qa_ref[...] = seg(_C_QA, _C_KA).astype(BF16)
    ka_ref[...] = seg(_C_KA, _C_VA).astype(BF16)
    va_ref[...] = seg(_C_VA, _C_QI).astype(BF16)
    qi_ref[...] = seg(_C_QI, _C_KI).astype(BF16)
    ki_ref[...] = seg(_C_KI, _C_WI).astype(BF16)
    wi_ref[...] = seg(_C_WI, _C_CQ)
    ct = ct_ref[...]
    st = st_ref[...]
    cq = _rms(seg(_C_CQ, _C_CKV), qn_ref[...]).astype(BF16)
    scale = (QK_NOPE + QK_ROPE) ** -0.5
    for hd in range(B_HEADS):
        sl = slice(hd * HB, (hd + 1) * HB)
        q = jnp.dot(cq, wuq_ref[:, sl], preferred_element_type=F32)
        qs = jnp.dot(cq, wuqs_ref[:, sl], preferred_element_type=F32)
        qb_ref[:, sl] = ((q * ct + qs * st) * scale).astype(BF16)
    ckv = _rms(seg(_C_CKV, _C_KR), kvn_ref[...]).astype(BF16)
    kr = seg(_C_KR, _C_KRS) * ct + seg(_C_KRS, _C_END) * st
    for hd in range(B_HEADS):
        sl = slice(hd * HB, (hd + 1) * HB)
        kb_ref[:, sl] = (jnp.dot(ckv, wukk_ref[:, sl], preferred_element_type=F32) + kr).astype(BF16)
    vb_ref[...] = jnp.dot(ckv, wukv_ref[...], preferred_element_type=F32).astype(BF16)


def _proj(x2, g, sc, sh, wp, qn, kvn, wuq, wuqs, wukk, wukv, ctab, stab, seq, tm=256):
    n, d = x2.shape
    tpb = seq // tm
    row = lambda w: pl.BlockSpec((tm, w), lambda i: (i, 0))
    full = lambda a: pl.BlockSpec(a.shape, lambda i: (0,) * a.ndim)
    per_b = pl.BlockSpec((1, 1, d), lambda i: (i // tpb, 0, 0))
    outs = [(D_A, BF16), (D_A, BF16), (D_A, BF16), (D_A, BF16), (LANES, BF16), (LANES, F32),
            (B_HEADS * HB, BF16), (B_HEADS * HB, BF16), (D_B, BF16)]
    return pl.pallas_call(
        _proj_kernel,
        grid=(n // tm,),
        in_specs=[row(d), full(g), per_b, per_b, full(wp), full(qn), full(kvn), full(wuq), full(wuqs),
                  full(wukk), full(wukv), row(HB), row(HB)],
        out_specs=[row(w) for w, _ in outs],
        out_shape=[jax.ShapeDtypeStruct((n, w), dt) for w, dt in outs],
        compiler_params=_cparams(("arbitrary",)),
        name="in_proj",
    )(x2, g, sc, sh, wp, qn, kvn, wuq, wuqs, wukk, wukv, ctab, stab)


def _pair_mask(x_pair, odd):
    lane = lax.broadcasted_iota(I32, x_pair.shape, 1)
    keep = (lane >= A_HEAD_DIM) if odd else (lane < A_HEAD_DIM)
    return jnp.where(keep, x_pair, jnp.zeros_like(x_pair))


def _flash_step(hd, s, v_pair, m_ref, l_ref, acc_ref):
    m_prev = m_ref[hd]
    m_new = jnp.maximum(m_prev, jnp.max(s, axis=1, keepdims=True))
    m_safe = jnp.where(m_new == NEG_INF, 0.0, m_new)
    alpha = jnp.exp(m_prev - m_safe)
    p = jnp.exp(s - m_safe[:, :1])
    l_ref[hd] = alpha * l_ref[hd] + jnp.sum(p, axis=1, keepdims=True)
    acc_ref[hd] = alpha * acc_ref[hd] + jnp.dot(p.astype(BF16), v_pair, preferred_element_type=F32)
    m_ref[hd] = m_new


def _flash_init(m_ref, l_ref, acc_ref):
    m_ref[...] = jnp.full(m_ref.shape, NEG_INF, F32)
    l_ref[...] = jnp.zeros(l_ref.shape, F32)
    acc_ref[...] = jnp.zeros(acc_ref.shape, F32)


def _flash_finish(y_ref, l_ref, acc_ref, n_heads):
    lane = lax.broadcasted_iota(I32, acc_ref.shape[1:], 1)
    for j in range(n_heads // 2):
        even = acc_ref[2 * j] / l_ref[2 * j]
        odd = acc_ref[2 * j + 1] / l_ref[2 * j + 1]
        y_ref[:, j * LANES:(j + 1) * LANES] = jnp.where(lane < V_DIM, even, odd).astype(y_ref.dtype)


def _t5_bias_tile(rel, bias_ref, hd):
    nb = N_BUCKETS // 2
    n = jnp.abs(rel)
    bucket = jnp.minimum(n, _T5_EXACT)
    for t in _T5_THRESH:
        bucket = bucket + (n >= t).astype(I32)
    bucket = bucket + jnp.where(rel > 0, nb, 0)
    out = jnp.full(rel.shape, bias_ref[0 * A_HEADS + hd], F32)
    for j in range(1, N_BUCKETS):
        out = jnp.where(bucket == j, bias_ref[j * A_HEADS + hd], out)
    return out


def _attn_a_kernel(nk_ref, qlo_ref, qhi_ref, klo_ref, khi_ref,
                   bias_ref, qa_ref, qi_ref, wi_ref, ka_ref, va_ref, ki_ref, pq_ref, pk_ref,
                   y_ref,
                   key_ref, am_ref, ex_ref, j_ref, m_ref, l_ref, acc_ref,
                   *, topk, tq, tk, nq, nkt):
    b = pl.program_id(0)
    qi_idx = pl.program_id(1)
    nk = nk_ref[b * nq + qi_idx]
    pq = pq_ref[...]
    qchunk = jnp.right_shift(pq, int(math.log2(CHUNK)))
    lane_tk = lax.broadcasted_iota(I32, (tq, tk), 1)

    def score_body(kt, carry):
        ks = ki_ref[pl.ds(pl.multiple_of(kt * tk, tk), tk), :]
        sc = jnp.zeros((tq, tk), F32)
        for hd in range(IDX_HEADS):
            pair = qi_ref[:, (hd // 2) * LANES:(hd // 2 + 1) * LANES]
            r = _dot_t(_pair_mask(pair, hd % 2), ks)
            sc = sc + wi_ref[:, hd:hd + 1] * jnp.maximum(r, 0.0)
        bits = pltpu.bitcast(sc, I32)
        key = bits ^ (jnp.right_shift(bits, 31) & jnp.int32(0x7FFFFFFF))
        kchunk = jnp.right_shift(pk_ref[kt], int(math.log2(CHUNK)))
        key_ref[kt] = jnp.where(kchunk <= qchunk, key, jnp.int32(INT_MIN))
        return carry
    lax.fori_loop(0, nk, score_body, 0)

    def count(pred):
        def body(kt, acc):
            return acc + pred(key_ref[kt], kt).astype(I32)
        acc = lax.fori_loop(0, nk, body, jnp.zeros((tq, tk), I32))
        return jnp.sum(acc, axis=1, keepdims=True)

    def bit_body(i, t):
        cand = t + jnp.left_shift(jnp.int32(1), 31 - i)
        c = count(lambda k, kt: k >= cand)
        return jnp.where(c >= topk, cand, t)
    thr = lax.fori_loop(0, 32, bit_body, jnp.full((tq, 1), INT_MIN, I32))
    need = topk - count(lambda k, kt: k > thr)
    n_eq = count(lambda k, kt: k == thr)
    real = thr > jnp.int32(INT_MIN)
    j_ref[...] = jnp.broadcast_to(jnp.where(real, jnp.int32(nkt * tk), jnp.int32(-1)), j_ref.shape)
    tie = jnp.max(jnp.where(real & (n_eq > need), 1, 0))

    @pl.when(tie > 0)
    def _():
        def idx_body(i, p):
            cand = p + jnp.left_shift(jnp.int32(1), (nkt * tk).bit_length() - 2 - i)
            c = count(lambda k, kt: (k == thr) & (lane_tk + kt * tk < cand))
            return jnp.where(c < need, cand, p)
        p = lax.fori_loop(0, (nkt * tk).bit_length() - 1, idx_body, jnp.zeros((tq, 1), I32))
        j_ref[...] = jnp.broadcast_to(jnp.where(real & (n_eq > need), p, j_ref[:, :1]), j_ref.shape)

    jsel = j_ref[:, :1]

    def mask_body(kt, carry):
        k = key_ref[kt]
        sel = (k > thr) | ((k == thr) & (lane_tk + kt * tk <= jsel))
        am_ref[kt] = jnp.where(sel, 0.0, NEG_INF)
        return carry
    lax.fori_loop(0, nk, mask_body, 0)

    _flash_init(m_ref, l_ref, acc_ref)
    q_lo = qlo_ref[b * nq + qi_idx]
    q_hi = qhi_ref[b * nq + qi_idx]

    def attn_body(kt, carry):
        start = pl.multiple_of(kt * tk, tk)
        am = am_ref[kt]
        max_rel = khi_ref[b * nkt + kt] - q_lo
        min_rel = klo_ref[b * nkt + kt] - q_hi
        far_past = max_rel <= -_T5_FAR
        far_future = min_rel >= _T5_FAR

        @pl.when(far_past)
        def _():
            for hd in range(A_HEADS):
                ex_ref[hd] = am + bias_ref[(N_BUCKETS // 2 - 1) * A_HEADS + hd]

        @pl.when(far_future)
        def _():
            for hd in range(A_HEADS):
                ex_ref[hd] = am + bias_ref[(N_BUCKETS - 1) * A_HEADS + hd]

        @pl.when(jnp.logical_not(far_past | far_future))
        def _():
            rel = pk_ref[kt] - pq
            for hd in range(A_HEADS):
                ex_ref[hd] = am + _t5_bias_tile(rel, bias_ref, hd)

        for hd in range(A_HEADS):
            pr = slice((hd // 2) * LANES, (hd // 2 + 1) * LANES)
            qm = _pair_mask(qa_ref[:, pr], hd % 2)
            s = _dot_t(qm, ka_ref[pl.ds(start, tk), pr]) + ex_ref[hd]
            _flash_step(hd, s, va_ref[pl.ds(start, tk), pr], m_ref, l_ref, acc_ref)
        return carry
    lax.fori_loop(0, nk, attn_body, 0)
    _flash_finish(y_ref, l_ref, acc_ref, A_HEADS)


def _tile_tables(positions, tq, tk):
    b, s = positions.shape
    pq = positions.reshape(b, s // tq, tq)
    pk = positions.reshape(b, s // tk, tk)
    q_lo, q_hi = pq.min(-1), pq.max(-1)
    k_lo, k_hi = pk.min(-1), pk.max(-1)
    vis = (k_lo // CHUNK)[:, None, :] <= (q_hi // CHUNK)[:, :, None]
    last = jnp.max(jnp.where(vis, jnp.arange(s // tk, dtype=I32)[None, None, :] + 1, 0), axis=-1)
    flat = lambda a: a.reshape(-1).astype(I32)
    return flat(last), flat(q_lo), flat(q_hi), flat(k_lo), flat(k_hi)


def _attn_a(qa, qi, wi, ka, va, ki, positions, rel_bias, tq=128, tk=256):
    b, s = positions.shape
    n = b * s
    nq, nkt = s // tq, s // tk
    topk = min(TOPK_MAX, s // 4)
    tables = _tile_tables(positions, tq, tk)
    pos_col = positions.reshape(n, 1)
    pos_row = positions.reshape(b, nkt, 1, tk)
    qrow = lambda w: pl.BlockSpec((tq, w), lambda bi, i, *_: (bi * nq + i, 0))
    kfull = lambda w: pl.BlockSpec((s, w), lambda bi, i, *_: (bi, 0))
    kern = functools.partial(_attn_a_kernel, topk=topk, tq=tq, tk=tk, nq=nq, nkt=nkt)
    grid_spec = pltpu.PrefetchScalarGridSpec(
        num_scalar_prefetch=5,
        grid=(b, nq),
        in_specs=[pl.BlockSpec(memory_space=pltpu.SMEM),
                  qrow(D_A), qrow(D_A), qrow(LANES), kfull(D_A), kfull(D_A), kfull(LANES),
                  pl.BlockSpec((tq, 1), lambda bi, i, *_: (bi * nq + i, 0)),
                  pl.BlockSpec((None, nkt, 1, tk), lambda bi, i, *_: (bi, 0, 0, 0))],
        out_specs=qrow(D_A),
        scratch_shapes=[pltpu.VMEM((nkt, tq, tk), I32), pltpu.VMEM((nkt, tq, tk), F32),
                        pltpu.VMEM((A_HEADS, tq, tk), F32), pltpu.VMEM((tq, LANES), I32),
                        pltpu.VMEM((A_HEADS, tq, LANES), F32), pltpu.VMEM((A_HEADS, tq, LANES), F32),
                        pltpu.VMEM((A_HEADS, tq, LANES), F32)])
    return pl.pallas_call(
        kern, grid_spec=grid_spec,
        out_shape=jax.ShapeDtypeStruct((n, D_A), BF16),
        compiler_params=_cparams(("arbitrary", "arbitrary")),
        name="attn_indexer",
    )(*tables, rel_bias.reshape(-1), qa, qi, wi, ka, va, ki, pos_col, pos_row)


def _attn_b_kernel(nk_ref, qb_ref, kb_ref, vb_ref, pq_ref, pk_ref, y_ref, m_ref, l_ref, acc_ref, *, tq, tk, nq):
    b = pl.program_id(0)
    nk = nk_ref[b * nq + pl.program_id(1)]
    qchunk = jnp.right_shift(pq_ref[...], int(math.log2(CHUNK)))
    _flash_init(m_ref, l_ref, acc_ref)

    def body(kt, carry):
        start = pl.multiple_of(kt * tk, tk)
        kchunk = jnp.right_shift(pk_ref[kt], int(math.log2(CHUNK)))
        am = jnp.where(kchunk <= qchunk, 0.0, NEG_INF)
        for hd in range(B_HEADS):
            hs = slice(hd * HB, (hd + 1) * HB)
            pr = slice((hd // 2) * LANES, (hd // 2 + 1) * LANES)
            s = _dot_t(qb_ref[:, hs], kb_ref[pl.ds(start, tk), hs]) + am
            _flash_step(hd, s, vb_ref[pl.ds(start, tk), pr], m_ref, l_ref, acc_ref)
        return carry
    lax.fori_loop(0, nk, body, 0)
    _flash_finish(y_ref, l_ref, acc_ref, B_HEADS)


def _attn_b(qb, kb, vb, positions, tq=256, tk=256):
    b, s = positions.shape
    n = b * s
    nq, nkt = s // tq, s // tk
    nk = _tile_tables(positions, tq, tk)[0]
    qrow = lambda w: pl.BlockSpec((tq, w), lambda bi, i, *_: (bi * nq + i, 0))
    kfull = lambda w: pl.BlockSpec((s, w), lambda bi, i, *_: (bi, 0))
    grid_spec = pltpu.PrefetchScalarGridSpec(
        num_scalar_prefetch=1,
        grid=(b, nq),
        in_specs=[qrow(B_HEADS * HB), kfull(B_HEADS * HB), kfull(D_B),
                  pl.BlockSpec((tq, 1), lambda bi, i, *_: (bi * nq + i, 0)),
                  pl.BlockSpec((None, nkt, 1, tk), lambda bi, i, *_: (bi, 0, 0, 0))],
        out_specs=qrow(D_B),
        scratch_shapes=[pltpu.VMEM((B_HEADS, tq, LANES), F32)] * 3)
    return pl.pallas_call(
        functools.partial(_attn_b_kernel, tq=tq, tk=tk, nq=nq), grid_spec=grid_spec,
        out_shape=jax.ShapeDtypeStruct((n, D_B), BF16),
        compiler_params=_cparams(("arbitrary", "arbitrary")),
        name="attn_latent",
    )(nk, qb, kb, vb, positions.reshape(n, 1), positions.reshape(b, nkt, 1, tk))


def _out_router_kernel(x_ref, ya_ref, yb_ref, wo_ref, gm_ref, g_ref, sc_ref, sh_ref, wr_ref, br_ref,
                       x1_ref, h2_ref, e_ref, gate_ref, *, n_experts):
    mix = (jnp.dot(ya_ref[...], wo_ref[:D_A, :], preferred_element_type=F32)
           + jnp.dot(yb_ref[...], wo_ref[D_A:, :], preferred_element_type=F32))
    x1 = x_ref[...] + gm_ref[0] * mix
    x1_ref[...] = x1
    h2 = _rms(x1, g_ref[...]) * (1.0 + sc_ref[0]) + sh_ref[0]
    h2_ref[...] = h2
    logits = jnp.dot(h2, wr_ref[...], preferred_element_type=F32, precision=lax.Precision.HIGHEST) + br_ref[...]
    lane = lax.broadcasted_iota(I32, logits.shape, 1)
    cur = jnp.where(lane < n_experts, logits, NEG_INF)
    e_out = jnp.zeros(logits.shape, I32)
    g_out = jnp.zeros(logits.shape, F32)
    top = None
    for k in range(TOP_K):
        m = jnp.max(cur, axis=1, keepdims=True)
        idx = jnp.min(jnp.where(cur == m, lane, LANES), axis=1, keepdims=True)
        top = m if top is None else top
        e_out = jnp.where(lane == k, idx, e_out)
        g_out = jnp.where(lane == k, jnp.exp(m - top), g_out)
        cur = jnp.where(lane == idx, NEG_INF, cur)
    e_ref[...] = e_out
    gate_ref[...] = g_out / jnp.sum(g_out, axis=1, keepdims=True)


def _out_router(x2, ya, yb, wo, gm, g, sc, sh, wr, br, seq, tm=256):
    n, d = x2.shape
    tpb = seq // tm
    n_experts = wr.shape[1]
    wr_p = jnp.pad(wr, ((0, 0), (0, LANES - n_experts)))
    br_p = jnp.pad(br, (0, LANES - n_experts)).reshape(1, LANES)
    row = lambda w: pl.BlockSpec((tm, w), lambda i: (i, 0))
    full = lambda a: pl.BlockSpec(a.shape, lambda i: (0,) * a.ndim)
    per_b = pl.BlockSpec((1, 1, d), lambda i: (i // tpb, 0, 0))
    return pl.pallas_call(
        functools.partial(_out_router_kernel, n_experts=n_experts),
        grid=(n // tm,),
        in_specs=[row(d), row(D_A), row(D_B), full(wo), per_b, full(g), per_b, per_b, full(wr_p), full(br_p)],
        out_specs=[row(d), row(d), row(LANES), row(LANES)],
        out_shape=[jax.ShapeDtypeStruct((n, d), F32), jax.ShapeDtypeStruct((n, d), F32),
                   jax.ShapeDtypeStruct((n, LANES), I32), jax.ShapeDtypeStruct((n, LANES), F32)],
        compiler_params=_cparams(("arbitrary",)),
        name="out_proj_router",
    )(x2, ya, yb, wo, gm, g, sc, sh, wr_p, br_p)


def _row_copy(src_hbm, row, dst_ref, slot, sem):
    return pltpu.make_async_copy(src_hbm.at[pl.ds(row, 1), :], dst_ref.at[pl.ds(slot, 1), :], sem)


def _moe_gather_kernel(tok_ref, h_hbm, o_ref, sem):
    base = pl.program_id(0) * MOE_BLOCK

    def issue(r, carry):
        _row_copy(h_hbm, tok_ref[base + r], o_ref, r, sem).start()
        return carry
    lax.fori_loop(0, MOE_BLOCK, issue, 0)
    pltpu.make_async_copy(h_hbm.at[pl.ds(0, MOE_BLOCK), :], o_ref, sem).wait()


def _moe_gather(buf_tok, h2):
    cap = buf_tok.shape[0]
    d = h2.shape[1]
    grid_spec = pltpu.PrefetchScalarGridSpec(
        num_scalar_prefetch=1, grid=(cap // MOE_BLOCK,),
        in_specs=[pl.BlockSpec(memory_space=pl.ANY)],
        out_specs=pl.BlockSpec((MOE_BLOCK, d), lambda i, *_: (i, 0)),
        scratch_shapes=[pltpu.SemaphoreType.DMA])
    return pl.pallas_call(
        _moe_gather_kernel, grid_spec=grid_spec,
        out_shape=jax.ShapeDtypeStruct((cap, d), h2.dtype),
        compiler_params=_cparams(("arbitrary",)),
        name="moe_gather",
    )(buf_tok, h2)


def _moe_expert_kernel(be_ref, nu_ref, xs_ref, w1_ref, b1_ref, w2_ref, b2_ref, y_ref, w1b_ref, w2b_ref, *, d_exp):
    i = pl.program_id(0)
    live = i < nu_ref[0]
    changed = jnp.logical_or(i == 0, be_ref[i] != be_ref[jnp.maximum(i - 1, 0)])

    @pl.when(live & changed)
    def _():
        w1b_ref[...] = w1_ref[0].astype(BF16)
        w2b_ref[...] = w2_ref[0].astype(BF16)

    @pl.when(live)
    def _():
        hid = jnp.dot(xs_ref[...].astype(BF16), w1b_ref[...], preferred_element_type=F32) + b1_ref[0]
        glu = jnp.minimum(hid[:, :d_exp], SWIGLU_LIMIT)
        lin = jnp.clip(hid[:, d_exp:], -SWIGLU_LIMIT, SWIGLU_LIMIT)
        act = glu * jax.nn.sigmoid(SWIGLU_ALPHA * glu) * (lin + 1.0)
        y_ref[...] = jnp.dot(act.astype(BF16), w2b_ref[...], preferred_element_type=F32) + b2_ref[0]

    @pl.when(jnp.logical_not(live))
    def _():
        y_ref[...] = jnp.zeros(y_ref.shape, y_ref.dtype)


def _moe_experts(block_e, n_used, xs, w1, b1, w2, b2):
    cap, d = xs.shape
    n_exp, _, d2 = w1.shape
    d_exp = d2 // 2
    grid_spec = pltpu.PrefetchScalarGridSpec(
        num_scalar_prefetch=2, grid=(cap // MOE_BLOCK,),
        in_specs=[pl.BlockSpec((MOE_BLOCK, d), lambda i, be, nu: (i, 0)),
                  pl.BlockSpec((1, d, d2), lambda i, be, nu: (be[i], 0, 0)),
                  pl.BlockSpec((1, 1, d2), lambda i, be, nu: (be[i], 0, 0)),
                  pl.BlockSpec((1, d_exp, d), lambda i, be, nu: (be[i], 0, 0)),
                  pl.BlockSpec((1, 1, d), lambda i, be, nu: (be[i], 0, 0))],
        out_specs=pl.BlockSpec((MOE_BLOCK, d), lambda i, be, nu: (i, 0)),
        scratch_shapes=[pltpu.VMEM((d, d2), BF16), pltpu.VMEM((d_exp, d), BF16)])
    return pl.pallas_call(
        functools.partial(_moe_expert_kernel, d_exp=d_exp), grid_spec=grid_spec,
        out_shape=jax.ShapeDtypeStruct((cap, d), F32),
        compiler_params=_cparams(("arbitrary",)),
        name="moe_experts",
    )(block_e, n_used, xs, w1, b1.reshape(n_exp, 1, d2), w2, b2.reshape(n_exp, 1, d))


def _moe_combine_kernel(dest_ref, ys_hbm, x1_ref, gate_ref, gf_ref, gfin_ref, o_ref, rows_ref, sem, *, tm, final):
    base = pl.program_id(0) * tm * TOP_K

    def issue(t, carry):
        for k in range(TOP_K):
            _row_copy(ys_hbm, dest_ref[base + t * TOP_K + k], rows_ref.at[k], t, sem).start()
        return carry
    lax.fori_loop(0, tm, issue, 0)
    for k in range(TOP_K):
        pltpu.make_async_copy(ys_hbm.at[pl.ds(0, tm), :], rows_ref.at[k], sem).wait()
    moe = gate_ref[:, 0:1] * rows_ref[0]
    for k in range(1, TOP_K):
        moe = moe + gate_ref[:, k:k + 1] * rows_ref[k]
    out = x1_ref[...] + gf_ref[0] * moe
    if final:
        out = _rms(out, gfin_ref[...])
    o_ref[...] = out


def _moe_combine(dest, ys, x1, gate, gf, g_final, seq, final, tm=256):
    n, d = x1.shape
    tpb = seq // tm
    grid_spec = pltpu.PrefetchScalarGridSpec(
        num_scalar_prefetch=1, grid=(n // tm,),
        in_specs=[pl.BlockSpec(memory_space=pl.ANY),
                  pl.BlockSpec((tm, d), lambda i, *_: (i, 0)),
                  pl.BlockSpec((tm, LANES), lambda i, *_: (i, 0)),
                  pl.BlockSpec((1, 1, d), lambda i, *_: (i // tpb, 0, 0)),
                  pl.BlockSpec((1, d), lambda i, *_: (0, 0))],
        out_specs=pl.BlockSpec((tm, d), lambda i, *_: (i, 0)),
        scratch_shapes=[pltpu.VMEM((TOP_K, tm, d), F32), pltpu.SemaphoreType.DMA])
    return pl.pallas_call(
        functools.partial(_moe_combine_kernel, tm=tm, final=final), grid_spec=grid_spec,
        out_shape=jax.ShapeDtypeStruct((n, d), F32),
        compiler_params=_cparams(("arbitrary",)),
        name="moe_combine",
    )(dest, ys, x1, gate, gf, g_final)


def _route_tables(top_e, n_experts):
    n = top_e.shape[0]
    nk = n * TOP_K
    n_blocks = -(-nk // MOE_BLOCK) + n_experts
    cap = n_blocks * MOE_BLOCK
    flat_e = top_e.reshape(nk)
    onehot = (flat_e[:, None] == jnp.arange(n_experts, dtype=I32)[None, :]).astype(I32)
    csum = jnp.cumsum(onehot, axis=0)
    rank = jnp.sum(csum * onehot, axis=1) - 1
    counts = csum[-1]
    padded = (counts + MOE_BLOCK - 1) // MOE_BLOCK * MOE_BLOCK
    pend = jnp.cumsum(padded)
    pstart = pend - padded
    dest = (pstart[flat_e] + rank).astype(I32)
    flat_tok = jnp.repeat(jnp.arange(n, dtype=I32), TOP_K)
    buf_tok = jnp.zeros((cap,), I32).at[dest].set(flat_tok)
    block_e = jnp.minimum(jnp.searchsorted(pend, jnp.arange(n_blocks, dtype=I32) * MOE_BLOCK, side='right'),
                          n_experts - 1).astype(I32)
    n_used = (pend[-1:] // MOE_BLOCK).astype(I32)
    return dest, buf_tok, block_e, n_used


def kernel(x, c, positions, rel_bias, norm_mix, w_ada, b_ada, w_in, q_norm, w_uq, kv_norm, w_ukv, w_out, norm_ffn, w_router, b_router, w1, b1, w2, b2, norm_final):
    bsz, seq, d = x.shape
    depth = w_ada.shape[0]
    n = bsz * seq
    n_experts = w_router.shape[-1]
    mod = _ada_mod(c, w_ada, b_ada)
    ctab, stab = _rope_tables(positions)
    row = lambda v: v.reshape(1, -1)
    x2 = x.reshape(n, d)
    for l in range(depth):
        sh_m, sc_m, g_m, sh_f, sc_f, g_f = [mod[l, :, i * d:(i + 1) * d].reshape(bsz, 1, d) for i in range(6)]
        wuq, wuqs = _pack_uq(w_uq[l])
        wukk, wukv = _pack_ukv(w_ukv[l])
        qa, ka, va, qi, ki, wi, qb, kb, vb = _proj(
            x2, row(norm_mix[l]), sc_m, sh_m, _pack_in_weights(w_in[l]), row(q_norm[l]), row(kv_norm[l]),
            wuq, wuqs, wukk, wukv, ctab, stab, seq)
        ya = _attn_a(qa, qi, wi, ka, va, ki, positions, rel_bias)
        yb = _attn_b(qb, kb, vb, positions)
        x1, h2, top_e, gate = _out_router(x2, ya, yb, w_out[l].astype(BF16), g_m, row(norm_ffn[l]), sc_f, sh_f,
                                          w_router[l], b_router[l], seq)
        dest, buf_tok, block_e, n_used = _route_tables(top_e[:, :TOP_K], n_experts)
        xs = _moe_gather(buf_tok, h2)
        ys = _moe_experts(block_e, n_used, xs, w1[l], b1[l], w2[l], b2[l])
        x2 = _moe_combine(dest, ys, x1, gate, g_f, row(norm_final), seq, final=(l == depth - 1))
    return x2.reshape(bsz, seq, d)
```

```python
import functools
import math

import numpy as np
import jax
import jax.numpy as jnp
from jax import lax
from jax.experimental import pallas as pl
from jax.experimental.pallas import tpu as pltpu

CHUNK = 64
EPS = 1e-6
A_HEADS = 8
A_HEAD_DIM = 64
IDX_HEADS = 8
IDX_DIM = 64
TOPK_MAX = 256
B_HEADS = 8
Q_LORA = 256
KV_LORA = 128
QK_NOPE = 64
QK_ROPE = 32
V_DIM = 64
ROPE_THETA = 10000.0
N_BUCKETS = 32
MAX_DISTANCE = 128
TOP_K = 4
SWIGLU_LIMIT = 7.0
SWIGLU_ALPHA = 1.702
MOE_BLOCK = 256

LANES = 128
VMEM_LIMIT = 56 * 1024 * 1024
INT_MIN = -2 ** 31
NEG_INF = float("-inf")
BIAS_ROWS = 32

F32 = jnp.float32
BF16 = jnp.bfloat16
I32 = jnp.int32

_T5_EXACT = (N_BUCKETS // 2) // 2
_T5_THRESH = tuple(
    int(math.ceil(_T5_EXACT * (MAX_DISTANCE / _T5_EXACT) ** (k / (N_BUCKETS // 2 - _T5_EXACT)) - 1e-9))
    for k in range(1, N_BUCKETS // 2 - _T5_EXACT))
_T5_FAR = _T5_THRESH[-1]


def _cparams(sem, vmem=VMEM_LIMIT):
    return pltpu.CompilerParams(dimension_semantics=sem, vmem_limit_bytes=vmem)


def _dot_t(a, b):
    return lax.dot_general(a, b, (((1,), (1,)), ((), ())), preferred_element_type=F32)


def _ada_kernel(c_ref, w_ref, b_ref, o_ref):
    c = c_ref[...]
    cond = c * jax.nn.sigmoid(c)
    o_ref[0] = jnp.dot(cond, w_ref[0], preferred_element_type=F32,
                       precision=lax.Precision.HIGHEST) + b_ref[0]


def _ada_mod(c, w_ada, b_ada, tn=512):
    depth, d, n6 = w_ada.shape
    b = c.shape[0]
    return pl.pallas_call(
        _ada_kernel,
        grid=(depth, n6 // tn),
        in_specs=[pl.BlockSpec((b, d), lambda l, j: (0, 0)),
                  pl.BlockSpec((1, d, tn), lambda l, j: (l, 0, j)),
                  pl.BlockSpec((1, 1, tn), lambda l, j: (l, 0, j))],
        out_specs=pl.BlockSpec((1, b, tn), lambda l, j: (l, 0, j)),
        out_shape=jax.ShapeDtypeStruct((depth, b, n6), F32),
        compiler_params=_cparams(("arbitrary", "arbitrary")),
        name="ada_mod",
    )(c, w_ada, b_ada.reshape(depth, 1, n6))


D_A = A_HEADS * A_HEAD_DIM
D_B = B_HEADS * V_DIM
HB = 128
_C_QA, _C_KA, _C_VA, _C_QI = 0, D_A, 2 * D_A, 3 * D_A
_C_KI = 4 * D_A
_C_WI = _C_KI + LANES
_C_CQ = _C_WI + LANES
_C_CKV = _C_CQ + Q_LORA
_C_KR = _C_CKV + KV_LORA
_C_KRS = _C_KR + LANES
_C_END = _C_KRS + LANES


def _pack_in_weights(w_in):
    d = w_in.shape[0]
    offs = np.cumsum((0, D_A, D_A, D_A, IDX_HEADS * IDX_DIM, IDX_DIM, IDX_HEADS, Q_LORA, KV_LORA, QK_ROPE))
    seg = [w_in[:, offs[i]:offs[i + 1]] for i in range(9)]
    q_a, k_a, v_a, q_i, k_i, w_i, c_q, c_kv, k_r = seg
    z = lambda n: jnp.zeros((d, n), w_in.dtype)
    half = QK_ROPE // 2
    k_rs = jnp.concatenate([k_r[:, half:], k_r[:, :half]], axis=1)
    cols = [q_a * (A_HEAD_DIM ** -0.5), k_a, v_a, q_i, k_i, k_i, w_i, z(LANES - IDX_HEADS), c_q, c_kv,
            z(QK_NOPE), k_r, z(HB - QK_NOPE - QK_ROPE), z(QK_NOPE), k_rs, z(HB - QK_NOPE - QK_ROPE)]
    return jnp.concatenate(cols, axis=1).astype(BF16)


def _pack_uq(w_uq):
    r = w_uq.shape[0]
    w = w_uq.reshape(r, B_HEADS, QK_NOPE + QK_ROPE)
    nope, rope = w[..., :QK_NOPE], w[..., QK_NOPE:]
    half = QK_ROPE // 2
    z = jnp.zeros((r, B_HEADS, HB - QK_NOPE - QK_ROPE), w.dtype)
    main = jnp.concatenate([nope, rope, z], axis=-1).reshape(r, B_HEADS * HB)
    swap = jnp.concatenate([jnp.zeros_like(nope), rope[..., half:], rope[..., :half], z], axis=-1)
    return main.astype(BF16), swap.reshape(r, B_HEADS * HB).astype(BF16)


def _pack_ukv(w_ukv):
    r = w_ukv.shape[0]
    w = w_ukv.reshape(r, B_HEADS, QK_NOPE + V_DIM)
    k = jnp.concatenate([w[..., :QK_NOPE], jnp.zeros((r, B_HEADS, HB - QK_NOPE), w.dtype)], axis=-1)
    v = w[..., QK_NOPE:]
    return k.reshape(r, B_HEADS * HB).astype(BF16), v.reshape(r, B_HEADS * V_DIM).astype(BF16)


def _rope_tables(positions):
    half = QK_ROPE // 2
    inv = ROPE_THETA ** (-jnp.arange(half, dtype=F32) / half)
    ang = positions.astype(F32).reshape(-1, 1) * inv[None, :]
    cos, sin = jnp.cos(ang), jnp.sin(ang)
    n = ang.shape[0]
    one = jnp.ones((n, QK_NOPE), F32)
    z = lambda k: jnp.zeros((n, k), F32)
    ctab = jnp.concatenate([one, cos, cos, z(HB - QK_NOPE - QK_ROPE)], axis=1)
    stab = jnp.concatenate([z(QK_NOPE), -sin, sin, z(HB - QK_NOPE - QK_ROPE)], axis=1)
    return ctab, stab


def _rms(x, g):
    return x * lax.rsqrt(jnp.mean(x * x, axis=-1, keepdims=True) + EPS) * g


def _proj_kernel(x_ref, g_ref, sc_ref, sh_ref, wp_ref, qn_ref, kvn_ref, wuq_ref, wuqs_ref, wukk_ref, wukv_ref,
                 ct_ref, st_ref,
                 qa_ref, ka_ref, va_ref, qi_ref, ki_ref, wi_ref, qb_ref, kb_ref, vb_ref):
    x = x_ref[...]
    h = _rms(x, g_ref[...]) * (1.0 + sc_ref[0]) + sh_ref[0]
    hb = h.astype(BF16)
    seg = lambda a, b: jnp.dot(hb, wp_ref[:, a:b], preferred_element_type=F32)
    qa_ref[...] = seg(_C_QA, _C_KA).astype(BF16)
    ka_ref[...] = seg(_C_KA, _C_VA).astype(BF16)
    va_ref[...] = seg(_C_VA, _C_QI).astype(BF16)
    qi_ref[...] = seg(_C_QI, _C_KI).astype(BF16)
    ki_ref[...] = seg(_C_KI, _C_WI).astype(BF16)
    wi_ref[...] = seg(_C_WI, _C_CQ)
    ct = ct_ref[...]
    st = st_ref[...]
    cq = _rms(seg(_C_CQ, _C_CKV), qn_ref[...]).astype(BF16)
    scale = (QK_NOPE + QK_ROPE) ** -0.5
    for hd in range(B_HEADS):
        sl = slice(hd * HB, (hd + 1) * HB)
        q = jnp.dot(cq, wuq_ref[:, sl], preferred_element_type=F32)
        qs = jnp.dot(cq, wuqs_ref[:, sl], preferred_element_type=F32)
        qb_ref[:, sl] = ((q * ct + qs * st) * scale).astype(BF16)
    ckv = _rms(seg(_C_CKV, _C_KR), kvn_ref[...]).astype(BF16)
    kr = seg(_C_KR, _C_KRS) * ct + seg(_C_KRS, _C_END) * st
    for hd in range(B_HEADS):
        sl = slice(hd * HB, (hd + 1) * HB)
        kb_ref[:, sl] = (jnp.dot(ckv, wukk_ref[:, sl], preferred_element_type=F32) + kr).astype(BF16)
    vb_ref[...] = jnp.dot(ckv, wukv_ref[...], preferred_element_type=F32).astype(BF16)


def _proj(x2, g, sc, sh, wp, qn, kvn, wuq, wuqs, wukk, wukv, ctab, stab, seq, tm=256):
    n, d = x2.shape
    tpb = seq // tm
    row = lambda w: pl.BlockSpec((tm, w), lambda i: (i, 0))
    full = lambda a: pl.BlockSpec(a.shape, lambda i: (0,) * a.ndim)
    per_b = pl.BlockSpec((1, 1, d), lambda i: (i // tpb, 0, 0))
    outs = [(D_A, BF16), (D_A, BF16), (D_A, BF16), (D_A, BF16), (LANES, BF16), (LANES, F32),
            (B_HEADS * HB, BF16), (B_HEADS * HB, BF16), (D_B, BF16)]
    return pl.pallas_call(
        _proj_kernel,
        grid=(n // tm,),
        in_specs=[row(d), full(g), per_b, per_b, full(wp), full(qn), full(kvn), full(wuq), full(wuqs),
                  full(wukk), full(wukv), row(HB), row(HB)],
        out_specs=[row(w) for w, _ in outs],
        out_shape=[jax.ShapeDtypeStruct((n, w), dt) for w, dt in outs],
        compiler_params=_cparams(("arbitrary",)),
        name="in_proj",
    )(x2, g, sc, sh, wp, qn, kvn, wuq, wuqs, wukk, wukv, ctab, stab)


def _pair_mask(x_pair, odd):
    lane = lax.broadcasted_iota(I32, x_pair.shape, 1)
    keep = (lane >= A_HEAD_DIM) if odd else (lane < A_HEAD_DIM)
    return jnp.where(keep, x_pair, jnp.zeros_like(x_pair))


def _flash_step(hd, s, v_pair, m_ref, l_ref, acc_ref):
    m_prev = m_ref[hd]
    m_new = jnp.maximum(m_prev, jnp.max(s, axis=1, keepdims=True))
    m_safe = jnp.where(m_new == NEG_INF, 0.0, m_new)
    alpha = jnp.exp(m_prev - m_safe)
    p = jnp.exp(s - m_safe[:, :1])
    l_ref[hd] = alpha * l_ref[hd] + jnp.sum(p, axis=1, keepdims=True)
    acc_ref[hd] = alpha * acc_ref[hd] + jnp.dot(p.astype(BF16), v_pair, preferred_element_type=F32)
    m_ref[hd] = m_new


def _flash_init(m_ref, l_ref, acc_ref):
    m_ref[...] = jnp.full(m_ref.shape, NEG_INF, F32)
    l_ref[...] = jnp.zeros(l_ref.shape, F32)
    acc_ref[...] = jnp.zeros(acc_ref.shape, F32)


def _flash_finish(y_ref, l_ref, acc_ref, n_heads):
    lane = lax.broadcasted_iota(I32, acc_ref.shape[1:], 1)
    for j in range(n_heads // 2):
        even = acc_ref[2 * j] / l_ref[2 * j]
        odd = acc_ref[2 * j + 1] / l_ref[2 * j + 1]
        y_ref[:, j * LANES:(j + 1) * LANES] = jnp.where(lane < V_DIM, even, odd).astype(y_ref.dtype)


def _t5_bias_heads(rel, bias_ref):
    nb = N_BUCKETS // 2
    n = jnp.abs(rel)
    bucket = jnp.minimum(n, _T5_EXACT)
    for t in _T5_THRESH:
        bucket = bucket + (n >= t).astype(I32)
    bucket = bucket + jnp.where(rel > 0, nb, 0)
    outs = [jnp.full(rel.shape, bias_ref[hd], F32) for hd in range(A_HEADS)]
    for j in range(1, N_BUCKETS):
        hit = bucket == j
        outs = [jnp.where(hit, bias_ref[j * A_HEADS + hd], outs[hd]) for hd in range(A_HEADS)]
    return outs


def _attn_a_kernel(nk_ref, qlo_ref, qhi_ref, klo_ref, khi_ref,
                   bias_ref, qa_ref, qi_ref, wi_ref, ka_ref, va_ref, ki_ref, pq_ref, pk_ref,
                   y_ref,
                   key_ref, am_ref, ex_ref, j_ref, m_ref, l_ref, acc_ref,
                   *, topk, tq, tk, nq, nkt):
    b = pl.program_id(0)
    qi_idx = pl.program_id(1)
    nk = nk_ref[b * nq + qi_idx]
    pq = pq_ref[...]
    qchunk = jnp.right_shift(pq, int(math.log2(CHUNK)))
    lane_tk = lax.broadcasted_iota(I32, (tq, tk), 1)

    def score_body(kt, carry):
        ks = ki_ref[pl.ds(pl.multiple_of(kt * tk, tk), tk), :]
        sc = jnp.zeros((tq, tk), F32)
        for hd in range(IDX_HEADS):
            pair = qi_ref[:, (hd // 2) * LANES:(hd // 2 + 1) * LANES]
            r = _dot_t(_pair_mask(pair, hd % 2), ks)
            sc = sc + wi_ref[:, hd:hd + 1] * jnp.maximum(r, 0.0)
        bits = pltpu.bitcast(sc, I32)
        key = bits ^ (jnp.right_shift(bits, 31) & jnp.int32(0x7FFFFFFF))
        kchunk = jnp.right_shift(pk_ref[kt], int(math.log2(CHUNK)))
        key_ref[kt] = jnp.where(kchunk <= qchunk, key, jnp.int32(INT_MIN))
        return carry
    lax.fori_loop(0, nk, score_body, 0)

    def count(pred):
        def body(kt, acc):
            return acc + pred(key_ref[kt], kt).astype(I32)
        acc = lax.fori_loop(0, nk, body, jnp.zeros((tq, tk), I32))
        return jnp.sum(acc, axis=1, keepdims=True)

    def bit_body(i, t):
        cand = t + jnp.left_shift(jnp.int32(1), 31 - i)
        c = count(lambda k, kt: k >= cand)
        return jnp.where(c >= topk, cand, t)
    thr = lax.fori_loop(0, 32, bit_body, jnp.full((tq, 1), INT_MIN, I32))
    need = topk - count(lambda k, kt: k > thr)
    n_eq = count(lambda k, kt: k == thr)
    real = thr > jnp.int32(INT_MIN)
    j_ref[...] = jnp.broadcast_to(jnp.where(real, jnp.int32(nkt * tk), jnp.int32(-1)), j_ref.shape)
    tie = jnp.max(jnp.where(real & (n_eq > need), 1, 0))

    @pl.when(tie > 0)
    def _():
        def idx_body(i, p):
            cand = p + jnp.left_shift(jnp.int32(1), (nkt * tk).bit_length() - 2 - i)
            c = count(lambda k, kt: (k == thr) & (lane_tk + kt * tk < cand))
            return jnp.where(c < need, cand, p)
        p = lax.fori_loop(0, (nkt * tk).bit_length() - 1, idx_body, jnp.zeros((tq, 1), I32))
        j_ref[...] = jnp.broadcast_to(jnp.where(real & (n_eq > need), p, j_ref[:, :1]), j_ref.shape)

    jsel = j_ref[:, :1]

    def mask_body(kt, carry):
        k = key_ref[kt]
        sel = (k > thr) | ((k == thr) & (lane_tk + kt * tk <= jsel))
        am_ref[kt] = jnp.where(sel, 0.0, NEG_INF)
        return carry
    lax.fori_loop(0, nk, mask_body, 0)

    _flash_init(m_ref, l_ref, acc_ref)
    q_lo = qlo_ref[b * nq + qi_idx]
    q_hi = qhi_ref[b * nq + qi_idx]

    def bias_chunk(kt, c):
        cs = slice(c * LANES, (c + 1) * LANES)
        chunk = b * (nkt * tk // LANES) + kt * (tk // LANES) + c
        far_past = khi_ref[chunk] - q_lo <= -_T5_FAR
        far_future = klo_ref[chunk] - q_hi >= _T5_FAR

        @pl.when(far_past)
        def _():
            for hd in range(A_HEADS):
                ex_ref[hd, :, cs] = am_ref[kt, :, cs] + bias_ref[(N_BUCKETS // 2 - 1) * A_HEADS + hd]

        @pl.when(far_future)
        def _():
            for hd in range(A_HEADS):
                ex_ref[hd, :, cs] = am_ref[kt, :, cs] + bias_ref[(N_BUCKETS - 1) * A_HEADS + hd]

        @pl.when(jnp.logical_not(far_past | far_future))
        def _():
            pkc = pk_ref[kt][:, cs]

            def rows_body(r, carry):
                rows = pl.ds(pl.multiple_of(r * BIAS_ROWS, BIAS_ROWS), BIAS_ROWS)
                am = am_ref[kt, rows, cs]
                for hd, bias in enumerate(_t5_bias_heads(pkc - pq_ref[rows, :], bias_ref)):
                    ex_ref[hd, rows, cs] = am + bias
                return carry
            lax.fori_loop(0, tq // BIAS_ROWS, rows_body, 0)

    def attn_body(kt, carry):
        start = pl.multiple_of(kt * tk, tk)
        for c in range(tk // LANES):
            bias_chunk(kt, c)

        for hd in range(A_HEADS):
            pr = slice((hd // 2) * LANES, (hd // 2 + 1) * LANES)
            qm = _pair_mask(qa_ref[:, pr], hd % 2)
            s = _dot_t(qm, ka_ref[pl.ds(start, tk), pr]) + ex_ref[hd]
            _flash_step(hd, s, va_ref[pl.ds(start, tk), pr], m_ref, l_ref, acc_ref)
        return carry
    lax.fori_loop(0, nk, attn_body, 0)
    _flash_finish(y_ref, l_ref, acc_ref, A_HEADS)


def _tile_tables(positions, tq, tk):
    b, s = positions.shape
    pq = positions.reshape(b, s // tq, tq)
    pk = positions.reshape(b, s // tk, tk)
    pc = positions.reshape(b, s // LANES, LANES)
    q_lo, q_hi = pq.min(-1), pq.max(-1)
    vis = (pk.min(-1) // CHUNK)[:, None, :] <= (q_hi // CHUNK)[:, :, None]
    last = jnp.max(jnp.where(vis, jnp.arange(s // tk, dtype=I32)[None, None, :] + 1, 0), axis=-1)
    flat = lambda a: a.reshape(-1).astype(I32)
    return flat(last), flat(q_lo), flat(q_hi), flat(pc.min(-1)), flat(pc.max(-1))


def _attn_a(qa, qi, wi, ka, va, ki, positions, rel_bias, tq=256, tk=512):
    b, s = positions.shape
    n = b * s
    nq, nkt = s // tq, s // tk
    topk = min(TOPK_MAX, s // 4)
    tables = _tile_tables(positions, tq, tk)
    pos_col = positions.reshape(n, 1)
    pos_row = positions.reshape(b, nkt, 1, tk)
    qrow = lambda w: pl.BlockSpec((tq, w), lambda bi, i, *_: (bi * nq + i, 0))
    kfull = lambda w: pl.BlockSpec((s, w), lambda bi, i, *_: (bi, 0))
    kern = functools.partial(_attn_a_kernel, topk=topk, tq=tq, tk=tk, nq=nq, nkt=nkt)
    grid_spec = pltpu.PrefetchScalarGridSpec(
        num_scalar_prefetch=5,
        grid=(b, nq),
        in_specs=[pl.BlockSpec(memory_space=pltpu.SMEM),
                  qrow(D_A), qrow(D_A), qrow(LANES), kfull(D_A), kfull(D_A), kfull(LANES),
                  pl.BlockSpec((tq, 1), lambda bi, i, *_: (bi * nq + i, 0)),
                  pl.BlockSpec((None, nkt, 1, tk), lambda bi, i, *_: (bi, 0, 0, 0))],
        out_specs=qrow(D_A),
        scratch_shapes=[pltpu.VMEM((nkt, tq, tk), I32), pltpu.VMEM((nkt, tq, tk), F32),
                        pltpu.VMEM((A_HEADS, tq, tk), F32), pltpu.VMEM((tq, LANES), I32),
                        pltpu.VMEM((A_HEADS, tq, LANES), F32), pltpu.VMEM((A_HEADS, tq, LANES), F32),
                        pltpu.VMEM((A_HEADS, tq, LANES), F32)])
    return pl.pallas_call(
        kern, grid_spec=grid_spec,
        out_shape=jax.ShapeDtypeStruct((n, D_A), BF16),
        compiler_params=_cparams(("arbitrary", "arbitrary")),
        name="attn_indexer",
    )(*tables, rel_bias.reshape(-1), qa, qi, wi, ka, va, ki, pos_col, pos_row)


def _attn_b_kernel(nk_ref, qb_ref, kb_ref, vb_ref, pq_ref, pk_ref, y_ref, m_ref, l_ref, acc_ref, *, tq, tk, nq):
    b = pl.program_id(0)
    nk = nk_ref[b * nq + pl.program_id(1)]
    qchunk = jnp.right_shift(pq_ref[...], int(math.log2(CHUNK)))
    _flash_init(m_ref, l_ref, acc_ref)

    def body(kt, carry):
        start = pl.multiple_of(kt * tk, tk)
        kchunk = jnp.right_shift(pk_ref[kt], int(math.log2(CHUNK)))
        am = jnp.where(kchunk <= qchunk, 0.0, NEG_INF)
        for hd in range(B_HEADS):
            hs = slice(hd * HB, (hd + 1) * HB)
            pr = slice((hd // 2) * LANES, (hd // 2 + 1) * LANES)
            s = _dot_t(qb_ref[:, hs], kb_ref[pl.ds(start, tk), hs]) + am
            _flash_step(hd, s, vb_ref[pl.ds(start, tk), pr], m_ref, l_ref, acc_ref)
        return carry
    lax.fori_loop(0, nk, body, 0)
    _flash_finish(y_ref, l_ref, acc_ref, B_HEADS)


def _attn_b(qb, kb, vb, positions, tq=256, tk=512):
    b, s = positions.shape
    n = b * s
    nq, nkt = s // tq, s // tk
    nk = _tile_tables(positions, tq, tk)[0]
    qrow = lambda w: pl.BlockSpec((tq, w), lambda bi, i, *_: (bi * nq + i, 0))
    kfull = lambda w: pl.BlockSpec((s, w), lambda bi, i, *_: (bi, 0))
    grid_spec = pltpu.PrefetchScalarGridSpec(
        num_scalar_prefetch=1,
        grid=(b, nq),
        in_specs=[qrow(B_HEADS * HB), kfull(B_HEADS * HB), kfull(D_B),
                  pl.BlockSpec((tq, 1), lambda bi, i, *_: (bi * nq + i, 0)),
                  pl.BlockSpec((None, nkt, 1, tk), lambda bi, i, *_: (bi, 0, 0, 0))],
        out_specs=qrow(D_B),
        scratch_shapes=[pltpu.VMEM((B_HEADS, tq, LANES), F32)] * 3)
    return pl.pallas_call(
        functools.partial(_attn_b_kernel, tq=tq, tk=tk, nq=nq), grid_spec=grid_spec,
        out_shape=jax.ShapeDtypeStruct((n, D_B), BF16),
        compiler_params=_cparams(("arbitrary", "arbitrary")),
        name="attn_latent",
    )(nk, qb, kb, vb, positions.reshape(n, 1), positions.reshape(b, nkt, 1, tk))


def _out_router_kernel(x_ref, ya_ref, yb_ref, wo_ref, gm_ref, g_ref, sc_ref, sh_ref, wr_ref, br_ref,
                       x1_ref, h2_ref, e_ref, gate_ref, *, n_experts):
    mix = (jnp.dot(ya_ref[...], wo_ref[:D_A, :], preferred_element_type=F32)
           + jnp.dot(yb_ref[...], wo_ref[D_A:, :], preferred_element_type=F32))
    x1 = x_ref[...] + gm_ref[0] * mix
    x1_ref[...] = x1
    h2 = _rms(x1, g_ref[...]) * (1.0 + sc_ref[0]) + sh_ref[0]
    h2_ref[...] = h2
    logits = jnp.dot(h2, wr_ref[...], preferred_element_type=F32, precision=lax.Precision.HIGHEST) + br_ref[...]
    lane = lax.broadcasted_iota(I32, logits.shape, 1)
    cur = jnp.where(lane < n_experts, logits, NEG_INF)
    e_out = jnp.zeros(logits.shape, I32)
    g_out = jnp.zeros(logits.shape, F32)
    top = None
    for k in range(TOP_K):
        m = jnp.max(cur, axis=1, keepdims=True)
        idx = jnp.min(jnp.where(cur == m, lane, LANES), axis=1, keepdims=True)
        top = m if top is None else top
        e_out = jnp.where(lane == k, idx, e_out)
        g_out = jnp.where(lane == k, jnp.exp(m - top), g_out)
        cur = jnp.where(lane == idx, NEG_INF, cur)
    e_ref[...] = e_out
    gate_ref[...] = g_out / jnp.sum(g_out, axis=1, keepdims=True)


def _out_router(x2, ya, yb, wo, gm, g, sc, sh, wr, br, seq, tm=256):
    n, d = x2.shape
    tpb = seq // tm
    n_experts = wr.shape[1]
    wr_p = jnp.pad(wr, ((0, 0), (0, LANES - n_experts)))
    br_p = jnp.pad(br, (0, LANES - n_experts)).reshape(1, LANES)
    row = lambda w: pl.BlockSpec((tm, w), lambda i: (i, 0))
    full = lambda a: pl.BlockSpec(a.shape, lambda i: (0,) * a.ndim)
    per_b = pl.BlockSpec((1, 1, d), lambda i: (i // tpb, 0, 0))
    return pl.pallas_call(
        functools.partial(_out_router_kernel, n_experts=n_experts),
        grid=(n // tm,),
        in_specs=[row(d), row(D_A), row(D_B), full(wo), per_b, full(g), per_b, per_b, full(wr_p), full(br_p)],
        out_specs=[row(d), row(d), row(LANES), row(LANES)],
        out_shape=[jax.ShapeDtypeStruct((n, d), F32), jax.ShapeDtypeStruct((n, d), F32),
                   jax.ShapeDtypeStruct((n, LANES), I32), jax.ShapeDtypeStruct((n, LANES), F32)],
        compiler_params=_cparams(("arbitrary",)),
        name="out_proj_router",
    )(x2, ya, yb, wo, gm, g, sc, sh, wr_p, br_p)


def _row_copy(src_hbm, row, dst_ref, slot, sem):
    return pltpu.make_async_copy(src_hbm.at[pl.ds(row, 1), :], dst_ref.at[pl.ds(slot, 1), :], sem)


def _moe_gather_kernel(tok_ref, h_hbm, o_ref, sem):
    base = pl.program_id(0) * MOE_BLOCK

    def issue(r, carry):
        _row_copy(h_hbm, tok_ref[base + r], o_ref, r, sem).start()
        return carry
    lax.fori_loop(0, MOE_BLOCK, issue, 0, unroll=8)
    pltpu.make_async_copy(h_hbm.at[pl.ds(0, MOE_BLOCK), :], o_ref, sem).wait()


def _moe_gather(buf_tok, h2):
    cap = buf_tok.shape[0]
    d = h2.shape[1]
    grid_spec = pltpu.PrefetchScalarGridSpec(
        num_scalar_prefetch=1, grid=(cap // MOE_BLOCK,),
        in_specs=[pl.BlockSpec(memory_space=pl.ANY)],
        out_specs=pl.BlockSpec((MOE_BLOCK, d), lambda i, *_: (i, 0)),
        scratch_shapes=[pltpu.SemaphoreType.DMA])
    return pl.pallas_call(
        _moe_gather_kernel, grid_spec=grid_spec,
        out_shape=jax.ShapeDtypeStruct((cap, d), h2.dtype),
        compiler_params=_cparams(("arbitrary",)),
        name="moe_gather",
    )(buf_tok, h2)


def _moe_expert_kernel(be_ref, nu_ref, xs_ref, w1_ref, b1_ref, w2_ref, b2_ref, y_ref, w1b_ref, w2b_ref, *, d_exp):
    i = pl.program_id(0)
    live = i < nu_ref[0]
    changed = jnp.logical_or(i == 0, be_ref[i] != be_ref[jnp.maximum(i - 1, 0)])

    @pl.when(live & changed)
    def _():
        w1b_ref[...] = w1_ref[0].astype(BF16)
        w2b_ref[...] = w2_ref[0].astype(BF16)

    @pl.when(live)
    def _():
        hid = jnp.dot(xs_ref[...].astype(BF16), w1b_ref[...], preferred_element_type=F32) + b1_ref[0]
        glu = jnp.minimum(hid[:, :d_exp], SWIGLU_LIMIT)
        lin = jnp.clip(hid[:, d_exp:], -SWIGLU_LIMIT, SWIGLU_LIMIT)
        act = glu * jax.nn.sigmoid(SWIGLU_ALPHA * glu) * (lin + 1.0)
        y_ref[...] = jnp.dot(act.astype(BF16), w2b_ref[...], preferred_element_type=F32) + b2_ref[0]

    @pl.when(jnp.logical_not(live))
    def _():
        y_ref[...] = jnp.zeros(y_ref.shape, y_ref.dtype)


def _moe_experts(block_e, n_used, xs, w1, b1, w2, b2):
    cap, d = xs.shape
    d2 = w1.shape[-1]
    d_exp = d2 // 2
    n_exp = w1.shape[0] * w1.shape[1]
    w1 = w1.reshape(n_exp, d, d2)
    w2 = w2.reshape(n_exp, d_exp, d)
    grid_spec = pltpu.PrefetchScalarGridSpec(
        num_scalar_prefetch=2, grid=(cap // MOE_BLOCK,),
        in_specs=[pl.BlockSpec((MOE_BLOCK, d), lambda i, be, nu: (i, 0)),
                  pl.BlockSpec((1, d, d2), lambda i, be, nu: (be[i], 0, 0)),
                  pl.BlockSpec((1, 1, d2), lambda i, be, nu: (be[i], 0, 0)),
                  pl.BlockSpec((1, d_exp, d), lambda i, be, nu: (be[i], 0, 0)),
                  pl.BlockSpec((1, 1, d), lambda i, be, nu: (be[i], 0, 0))],
        out_specs=pl.BlockSpec((MOE_BLOCK, d), lambda i, be, nu: (i, 0)),
        scratch_shapes=[pltpu.VMEM((d, d2), BF16), pltpu.VMEM((d_exp, d), BF16)])
    return pl.pallas_call(
        functools.partial(_moe_expert_kernel, d_exp=d_exp), grid_spec=grid_spec,
        out_shape=jax.ShapeDtypeStruct((cap, d), F32),
        compiler_params=_cparams(("arbitrary",)),
        name="moe_experts",
    )(block_e, n_used, xs, w1, b1.reshape(n_exp, 1, d2), w2, b2.reshape(n_exp, 1, d))


def _moe_combine_kernel(dest_ref, ys_hbm, x1_ref, gate_ref, gf_ref, gfin_ref, o_ref, rows_ref, sem, *, tm, final):
    base = pl.program_id(0) * tm * TOP_K

    def issue(t, carry):
        for k in range(TOP_K):
            _row_copy(ys_hbm, dest_ref[base + t * TOP_K + k], rows_ref.at[k], t, sem).start()
        return carry
    lax.fori_loop(0, tm, issue, 0)
    for k in range(TOP_K):
        pltpu.make_async_copy(ys_hbm.at[pl.ds(0, tm), :], rows_ref.at[k], sem).wait()
    moe = gate_ref[:, 0:1] * rows_ref[0]
    for k in range(1, TOP_K):
        moe = moe + gate_ref[:, k:k + 1] * rows_ref[k]
    out = x1_ref[...] + gf_ref[0] * moe
    if final:
        out = _rms(out, gfin_ref[...])
    o_ref[...] = out


def _moe_combine(dest, ys, x1, gate, gf, g_final, seq, final, tm=256):
    n, d = x1.shape
    tpb = seq // tm
    grid_spec = pltpu.PrefetchScalarGridSpec(
        num_scalar_prefetch=1, grid=(n // tm,),
        in_specs=[pl.BlockSpec(memory_space=pl.ANY),
                  pl.BlockSpec((tm, d), lambda i, *_: (i, 0)),
                  pl.BlockSpec((tm, LANES), lambda i, *_: (i, 0)),
                  pl.BlockSpec((1, 1, d), lambda i, *_: (i // tpb, 0, 0)),
                  pl.BlockSpec((1, d), lambda i, *_: (0, 0))],
        out_specs=pl.BlockSpec((tm, d), lambda i, *_: (i, 0)),
        scratch_shapes=[pltpu.VMEM((TOP_K, tm, d), F32), pltpu.SemaphoreType.DMA])
    return pl.pallas_call(
        functools.partial(_moe_combine_kernel, tm=tm, final=final), grid_spec=grid_spec,
        out_shape=jax.ShapeDtypeStruct((n, d), F32),
        compiler_params=_cparams(("arbitrary",)),
        name="moe_combine",
    )(dest, ys, x1, gate, gf, g_final)


def _route_tables(top_e, n_experts):
    n = top_e.shape[0]
    nk = n * TOP_K
    n_blocks = -(-nk // MOE_BLOCK) + n_experts
    cap = n_blocks * MOE_BLOCK
    flat_e = top_e.reshape(nk)
    onehot = (flat_e[:, None] == jnp.arange(n_experts, dtype=I32)[None, :]).astype(I32)
    csum = jnp.cumsum(onehot, axis=0)
    rank = jnp.sum(csum * onehot, axis=1) - 1
    counts = csum[-1]
    padded = (counts + MOE_BLOCK - 1) // MOE_BLOCK * MOE_BLOCK
    pend = jnp.cumsum(padded)
    pstart = pend - padded
    dest = (pstart[flat_e] + rank).astype(I32)
    flat_tok = jnp.repeat(jnp.arange(n, dtype=I32), TOP_K)
    buf_tok = jnp.zeros((cap,), I32).at[dest].set(flat_tok)
    block_e = jnp.minimum(jnp.searchsorted(pend, jnp.arange(n_blocks, dtype=I32) * MOE_BLOCK, side='right'),
                          n_experts - 1).astype(I32)
    n_used = (pend[-1:] // MOE_BLOCK).astype(I32)
    return dest, buf_tok, block_e, n_used


def kernel(x, c, positions, rel_bias, norm_mix, w_ada, b_ada, w_in, q_norm, w_uq, kv_norm, w_ukv, w_out, norm_ffn, w_router, b_router, w1, b1, w2, b2, norm_final):
    bsz, seq, d = x.shape
    depth = w_ada.shape[0]
    n = bsz * seq
    n_experts = w_router.shape[-1]
    mod = _ada_mod(c, w_ada, b_ada)
    ctab, stab = _rope_tables(positions)
    row = lambda v: v.reshape(1, -1)
    x2 = x.reshape(n, d)
    for l in range(depth):
        sh_m, sc_m, g_m, sh_f, sc_f, g_f = [mod[l, :, i * d:(i + 1) * d].reshape(bsz, 1, d) for i in range(6)]
        wuq, wuqs = _pack_uq(w_uq[l])
        wukk, wukv = _pack_ukv(w_ukv[l])
        qa, ka, va, qi, ki, wi, qb, kb, vb = _proj(
            x2, row(norm_mix[l]), sc_m, sh_m, _pack_in_weights(w_in[l]), row(q_norm[l]), row(kv_norm[l]),
            wuq, wuqs, wukk, wukv, ctab, stab, seq)
        ya = _attn_a(qa, qi, wi, ka, va, ki, positions, rel_bias)
        yb = _attn_b(qb, kb, vb, positions)
        x1, h2, top_e, gate = _out_router(x2, ya, yb, w_out[l].astype(BF16), g_m, row(norm_ffn[l]), sc_f, sh_f,
                                          w_router[l], b_router[l], seq)
        dest, buf_tok, block_e, n_used = _route_tables(top_e[:, :TOP_K], n_experts)
        xs = _moe_gather(buf_tok, h2)
        ys = _moe_experts(block_e + l * n_experts, n_used, xs, w1, b1, w2, b2)
        x2 = _moe_combine(dest, ys, x1, gate, g_f, row(norm_final), seq, final=(l == depth - 1))
    return x2.reshape(bsz, seq, d)
```

```python
import functools
import math

import numpy as np
import jax
import jax.numpy as jnp
from jax import lax
from jax.experimental import pallas as pl
from jax.experimental.pallas import tpu as pltpu

CHUNK = 64
EPS = 1e-6
A_HEADS = 8
A_HEAD_DIM = 64
IDX_HEADS = 8
IDX_DIM = 64
TOPK_MAX = 256
B_HEADS = 8
Q_LORA = 256
KV_LORA = 128
QK_NOPE = 64
QK_ROPE = 32
V_DIM = 64
ROPE_THETA = 10000.0
N_BUCKETS = 32
MAX_DISTANCE = 128
TOP_K = 4
SWIGLU_LIMIT = 7.0
SWIGLU_ALPHA = 1.702
MOE_BLOCK = 256

LANES = 128
VMEM_LIMIT = 56 * 1024 * 1024
INT_MIN = -2 ** 31
NEG_INF = float("-inf")
BIAS_ROWS = 32

F32 = jnp.float32
BF16 = jnp.bfloat16
I32 = jnp.int32

_T5_EXACT = (N_BUCKETS // 2) // 2
_T5_THRESH = tuple(
    int(math.ceil(_T5_EXACT * (MAX_DISTANCE / _T5_EXACT) ** (k / (N_BUCKETS // 2 - _T5_EXACT)) - 1e-9))
    for k in range(1, N_BUCKETS // 2 - _T5_EXACT))
_T5_FAR = _T5_THRESH[-1]


def _cparams(sem, vmem=VMEM_LIMIT):
    return pltpu.CompilerParams(dimension_semantics=sem, vmem_limit_bytes=vmem)


def _dot_t(a, b):
    return lax.dot_general(a, b, (((1,), (1,)), ((), ())), preferred_element_type=F32)


def _ada_kernel(c_ref, w_ref, b_ref, o_ref):
    c = c_ref[...]
    cond = c * jax.nn.sigmoid(c)
    o_ref[0] = jnp.dot(cond, w_ref[0], preferred_element_type=F32,
                       precision=lax.Precision.HIGHEST) + b_ref[0]


def _ada_mod(c, w_ada, b_ada, tn=512):
    depth, d, n6 = w_ada.shape
    b = c.shape[0]
    return pl.pallas_call(
        _ada_kernel,
        grid=(depth, n6 // tn),
        in_specs=[pl.BlockSpec((b, d), lambda l, j: (0, 0)),
                  pl.BlockSpec((1, d, tn), lambda l, j: (l, 0, j)),
                  pl.BlockSpec((1, 1, tn), lambda l, j: (l, 0, j))],
        out_specs=pl.BlockSpec((1, b, tn), lambda l, j: (l, 0, j)),
        out_shape=jax.ShapeDtypeStruct((depth, b, n6), F32),
        compiler_params=_cparams(("arbitrary", "arbitrary")),
        name="ada_mod",
    )(c, w_ada, b_ada.reshape(depth, 1, n6))


D_A = A_HEADS * A_HEAD_DIM
D_B = B_HEADS * V_DIM
HB = 128
_C_QA, _C_KA, _C_VA, _C_QI = 0, D_A, 2 * D_A, 3 * D_A
_C_KI = 4 * D_A
_C_WI = _C_KI + LANES
_C_CQ = _C_WI + LANES
_C_CKV = _C_CQ + Q_LORA
_C_KR = _C_CKV + KV_LORA
_C_KRS = _C_KR + LANES
_C_END = _C_KRS + LANES


def _pack_in_weights(w_in):
    d = w_in.shape[0]
    offs = np.cumsum((0, D_A, D_A, D_A, IDX_HEADS * IDX_DIM, IDX_DIM, IDX_HEADS, Q_LORA, KV_LORA, QK_ROPE))
    seg = [w_in[:, offs[i]:offs[i + 1]] for i in range(9)]
    q_a, k_a, v_a, q_i, k_i, w_i, c_q, c_kv, k_r = seg
    z = lambda n: jnp.zeros((d, n), w_in.dtype)
    half = QK_ROPE // 2
    k_rs = jnp.concatenate([k_r[:, half:], k_r[:, :half]], axis=1)
    cols = [q_a * (A_HEAD_DIM ** -0.5), k_a, v_a, q_i, k_i, k_i, w_i, z(LANES - IDX_HEADS), c_q, c_kv,
            z(QK_NOPE), k_r, z(HB - QK_NOPE - QK_ROPE), z(QK_NOPE), k_rs, z(HB - QK_NOPE - QK_ROPE)]
    return jnp.concatenate(cols, axis=1).astype(BF16)


def _pack_uq(w_uq):
    r = w_uq.shape[0]
    w = w_uq.reshape(r, B_HEADS, QK_NOPE + QK_ROPE)
    nope, rope = w[..., :QK_NOPE], w[..., QK_NOPE:]
    half = QK_ROPE // 2
    z = jnp.zeros((r, B_HEADS, HB - QK_NOPE - QK_ROPE), w.dtype)
    main = jnp.concatenate([nope, rope, z], axis=-1).reshape(r, B_HEADS * HB)
    swap = jnp.concatenate([jnp.zeros_like(nope), rope[..., half:], rope[..., :half], z], axis=-1)
    return main.astype(BF16), swap.reshape(r, B_HEADS * HB).astype(BF16)


def _pack_ukv(w_ukv):
    r = w_ukv.shape[0]
    w = w_ukv.reshape(r, B_HEADS, QK_NOPE + V_DIM)
    k = jnp.concatenate([w[..., :QK_NOPE], jnp.zeros((r, B_HEADS, HB - QK_NOPE), w.dtype)], axis=-1)
    v = w[..., QK_NOPE:]
    return k.reshape(r, B_HEADS * HB).astype(BF16), v.reshape(r, B_HEADS * V_DIM).astype(BF16)


def _rope_tables(positions):
    half = QK_ROPE // 2
    inv = ROPE_THETA ** (-jnp.arange(half, dtype=F32) / half)
    ang = positions.astype(F32).reshape(-1, 1) * inv[None, :]
    cos, sin = jnp.cos(ang), jnp.sin(ang)
    n = ang.shape[0]
    one = jnp.ones((n, QK_NOPE), F32)
    z = lambda k: jnp.zeros((n, k), F32)
    ctab = jnp.concatenate([one, cos, cos, z(HB - QK_NOPE - QK_ROPE)], axis=1)
    stab = jnp.concatenate([z(QK_NOPE), -sin, sin, z(HB - QK_NOPE - QK_ROPE)], axis=1)
    return ctab, stab


def _rms(x, g):
    return x * lax.rsqrt(jnp.mean(x * x, axis=-1, keepdims=True) + EPS) * g


def _proj_kernel(x_ref, g_ref, sc_ref, sh_ref, wp_ref, qn_ref, kvn_ref, wuq_ref, wuqs_ref, wukk_ref, wukv_ref,
                 ct_ref, st_ref,
                 qa_ref, ka_ref, va_ref, qi_ref, ki_ref, wi_ref, qb_ref, kb_ref, vb_ref):
    x = x_ref[...]
    h = _rms(x, g_ref[...]) * (1.0 + sc_ref[0]) + sh_ref[0]
    hb = h.astype(BF16)
    seg = lambda a, b: jnp.dot(hb, wp_ref[:, a:b], preferred_element_type=F32)
    qa_ref[...] = seg(_C_QA, _C_KA).astype(BF16)
    ka_ref[...] = seg(_C_KA, _C_VA).astype(BF16)
    va_ref[...] = seg(_C_VA, _C_QI).astype(BF16)
    qi_ref[...] = seg(_C_QI, _C_KI).astype(BF16)
    ki_ref[...] = seg(_C_KI, _C_WI).astype(BF16)
    wi_ref[...] = seg(_C_WI, _C_CQ)
    ct = ct_ref[...]
    st = st_ref[...]
    cq = _rms(seg(_C_CQ, _C_CKV), qn_ref[...]).astype(BF16)
    scale = (QK_NOPE + QK_ROPE) ** -0.5
    for hd in range(B_HEADS):
        sl = slice(hd * HB, (hd + 1) * HB)
        q = jnp.dot(cq, wuq_ref[:, sl], preferred_element_type=F32)
        qs = jnp.dot(cq, wuqs_ref[:, sl], preferred_element_type=F32)
        qb_ref[:, sl] = ((q * ct + qs * st) * scale).astype(BF16)
    ckv = _rms(seg(_C_CKV, _C_KR), kvn_ref[...]).astype(BF16)
    kr = seg(_C_KR, _C_KRS) * ct + seg(_C_KRS, _C_END) * st
    for hd in range(B_HEADS):
        sl = slice(hd * HB, (hd + 1) * HB)
        kb_ref[:, sl] = (jnp.dot(ckv, wukk_ref[:, sl], preferred_element_type=F32) + kr).astype(BF16)
    vb_ref[...] = jnp.dot(ckv, wukv_ref[...], preferred_element_type=F32).astype(BF16)


def _proj(x2, g, sc, sh, wp, qn, kvn, wuq, wuqs, wukk, wukv, ctab, stab, seq, tm=256):
    n, d = x2.shape
    tpb = seq // tm
    row = lambda w: pl.BlockSpec((tm, w), lambda i: (i, 0))
    full = lambda a: pl.BlockSpec(a.shape, lambda i: (0,) * a.ndim)
    per_b = pl.BlockSpec((1, 1, d), lambda i: (i // tpb, 0, 0))
    outs = [(D_A, BF16), (D_A, BF16), (D_A, BF16), (D_A, BF16), (LANES, BF16), (LANES, F32),
            (B_HEADS * HB, BF16), (B_HEADS * HB, BF16), (D_B, BF16)]
    return pl.pallas_call(
        _proj_kernel,
        grid=(n // tm,),
        in_specs=[row(d), full(g), per_b, per_b, full(wp), full(qn), full(kvn), full(wuq), full(wuqs),
                  full(wukk), full(wukv), row(HB), row(HB)],
        out_specs=[row(w) for w, _ in outs],
        out_shape=[jax.ShapeDtypeStruct((n, w), dt) for w, dt in outs],
        compiler_params=_cparams(("arbitrary",)),
        name="in_proj",
    )(x2, g, sc, sh, wp, qn, kvn, wuq, wuqs, wukk, wukv, ctab, stab)


def _pair_mask(x_pair, odd):
    lane = lax.broadcasted_iota(I32, x_pair.shape, 1)
    keep = (lane >= A_HEAD_DIM) if odd else (lane < A_HEAD_DIM)
    return jnp.where(keep, x_pair, jnp.zeros_like(x_pair))


def _flash_step(hd, s, v_pair, m_ref, l_ref, acc_ref):
    m_prev = m_ref[hd]
    m_new = jnp.maximum(m_prev, jnp.max(s, axis=1, keepdims=True))
    m_safe = jnp.where(m_new == NEG_INF, 0.0, m_new)
    alpha = jnp.exp(m_prev - m_safe)
    p = jnp.exp(s - jnp.concatenate([m_safe] * (s.shape[1] // LANES), axis=1))
    l_ref[hd] = alpha * l_ref[hd] + jnp.sum(p, axis=1, keepdims=True)
    acc_ref[hd] = alpha * acc_ref[hd] + jnp.dot(p.astype(BF16), v_pair, preferred_element_type=F32)
    m_ref[hd] = m_new


def _flash_init(m_ref, l_ref, acc_ref):
    m_ref[...] = jnp.full(m_ref.shape, NEG_INF, F32)
    l_ref[...] = jnp.zeros(l_ref.shape, F32)
    acc_ref[...] = jnp.zeros(acc_ref.shape, F32)


def _flash_finish(y_ref, l_ref, acc_ref, n_heads):
    lane = lax.broadcasted_iota(I32, acc_ref.shape[1:], 1)
    for j in range(n_heads // 2):
        even = acc_ref[2 * j] / l_ref[2 * j]
        odd = acc_ref[2 * j + 1] / l_ref[2 * j + 1]
        y_ref[:, j * LANES:(j + 1) * LANES] = jnp.where(lane < V_DIM, even, odd).astype(y_ref.dtype)


def _t5_bias_heads(rel, bias_ref):
    nb = N_BUCKETS // 2
    n = jnp.abs(rel)
    bucket = jnp.minimum(n, _T5_EXACT)
    for t in _T5_THRESH:
        bucket = bucket + (n >= t).astype(I32)
    bucket = bucket + jnp.where(rel > 0, nb, 0)
    outs = [jnp.full(rel.shape, bias_ref[hd], F32) for hd in range(A_HEADS)]
    for j in range(1, N_BUCKETS):
        hit = bucket == j
        outs = [jnp.where(hit, bias_ref[j * A_HEADS + hd], outs[hd]) for hd in range(A_HEADS)]
    return outs


def _attn_a_kernel(nk_ref, qlo_ref, qhi_ref, klo_ref, khi_ref,
                   bias_ref, qa_ref, qi_ref, wi_ref, ka_ref, va_ref, ki_ref, pq_ref, pk_ref,
                   y_ref,
                   key_ref, am_ref, ex_ref, j_ref, wb_ref, m_ref, l_ref, acc_ref,
                   *, topk, tq, tk, nq, nkt):
    b = pl.program_id(0)
    qi_idx = pl.program_id(1)
    nk = nk_ref[b * nq + qi_idx]
    pq = pq_ref[...]
    qchunk = jnp.right_shift(pq, int(math.log2(CHUNK)))
    lane_tk = lax.broadcasted_iota(I32, (tq, tk), 1)

    wide = lambda v: jnp.concatenate([v] * (tk // LANES), axis=1)
    for hd in range(IDX_HEADS):
        wb_ref[hd] = jnp.broadcast_to(wi_ref[:, hd:hd + 1], (tq, LANES))

    def score_body(kt, carry):
        ks = ki_ref[pl.ds(pl.multiple_of(kt * tk, tk), tk), :]
        sc = jnp.zeros((tq, tk), F32)
        for hd in range(IDX_HEADS):
            pair = qi_ref[:, (hd // 2) * LANES:(hd // 2 + 1) * LANES]
            r = _dot_t(_pair_mask(pair, hd % 2), ks)
            sc = sc + wide(wb_ref[hd]) * jnp.maximum(r, 0.0)
        bits = pltpu.bitcast(sc, I32)
        key = bits ^ (jnp.right_shift(bits, 31) & jnp.int32(0x7FFFFFFF))
        kchunk = jnp.right_shift(pk_ref[kt], int(math.log2(CHUNK)))
        key_ref[kt] = jnp.where(kchunk <= qchunk, key, jnp.int32(INT_MIN))
        return carry
    lax.fori_loop(0, nk, score_body, 0)

    ones = jnp.ones((LANES, LANES), BF16)

    def count(pred):
        def body(kt, acc):
            hit = jnp.where(pred(key_ref[kt], kt), 1, 0)
            for c in range(tk // LANES):
                acc = acc + hit[:, c * LANES:(c + 1) * LANES]
            return acc
        part = lax.fori_loop(0, nk, body, jnp.zeros((tq, LANES), I32))
        return jnp.dot(part.astype(F32).astype(BF16), ones, preferred_element_type=F32)

    def bit_body(i, t):
        cand = t + jnp.left_shift(jnp.int32(1), 31 - i)
        cand_w = wide(cand)
        c = count(lambda k, kt: k >= cand_w)
        return jnp.where(c >= topk, cand, t)
    thr = lax.fori_loop(0, 32, bit_body, jnp.full((tq, LANES), INT_MIN, I32))
    thr_w = wide(thr)
    need = topk - count(lambda k, kt: k > thr_w)
    n_eq = count(lambda k, kt: k == thr_w)
    real = thr > jnp.int32(INT_MIN)
    j_ref[...] = jnp.where(real, jnp.int32(nkt * tk), jnp.int32(-1))
    tie = jnp.max(jnp.where(real & (n_eq > need), 1, 0))

    @pl.when(tie > 0)
    def _():
        def idx_body(i, p):
            cand = p + jnp.left_shift(jnp.int32(1), (nkt * tk).bit_length() - 2 - i)
            cand_w = wide(cand)
            c = count(lambda k, kt: (k == thr_w) & (lane_tk + kt * tk < cand_w))
            return jnp.where(c < need, cand, p)
        p = lax.fori_loop(0, (nkt * tk).bit_length() - 1, idx_body, jnp.zeros((tq, LANES), I32))
        j_ref[...] = jnp.where(real & (n_eq > need), p, j_ref[...])

    jsel_w = wide(j_ref[...])

    def mask_body(kt, carry):
        k = key_ref[kt]
        sel = (k > thr_w) | ((k == thr_w) & (lane_tk + kt * tk <= jsel_w))
        am_ref[kt] = jnp.where(sel, 0.0, NEG_INF)
        return carry
    lax.fori_loop(0, nk, mask_body, 0)

    _flash_init(m_ref, l_ref, acc_ref)
    q_lo = qlo_ref[b * nq + qi_idx]
    q_hi = qhi_ref[b * nq + qi_idx]

    def bias_chunk(kt, c):
        cs = slice(c * LANES, (c + 1) * LANES)
        chunk = b * (nkt * tk // LANES) + kt * (tk // LANES) + c
        far_past = khi_ref[chunk] - q_lo <= -_T5_FAR
        far_future = klo_ref[chunk] - q_hi >= _T5_FAR

        @pl.when(far_past)
        def _():
            for hd in range(A_HEADS):
                ex_ref[hd, :, cs] = am_ref[kt, :, cs] + bias_ref[(N_BUCKETS // 2 - 1) * A_HEADS + hd]

        @pl.when(far_future)
        def _():
            for hd in range(A_HEADS):
                ex_ref[hd, :, cs] = am_ref[kt, :, cs] + bias_ref[(N_BUCKETS - 1) * A_HEADS + hd]

        @pl.when(jnp.logical_not(far_past | far_future))
        def _():
            pkc = pk_ref[kt][:, cs]

            def rows_body(r, carry):
                rows = pl.ds(pl.multiple_of(r * BIAS_ROWS, BIAS_ROWS), BIAS_ROWS)
                am = am_ref[kt, rows, cs]
                for hd, bias in enumerate(_t5_bias_heads(pkc - pq_ref[rows, :], bias_ref)):
                    ex_ref[hd, rows, cs] = am + bias
                return carry
            lax.fori_loop(0, tq // BIAS_ROWS, rows_body, 0)

    def attn_body(kt, carry):
        start = pl.multiple_of(kt * tk, tk)
        for c in range(tk // LANES):
            bias_chunk(kt, c)

        for hd in range(A_HEADS):
            pr = slice((hd // 2) * LANES, (hd // 2 + 1) * LANES)
            qm = _pair_mask(qa_ref[:, pr], hd % 2)
            s = _dot_t(qm, ka_ref[pl.ds(start, tk), pr]) + ex_ref[hd]
            _flash_step(hd, s, va_ref[pl.ds(start, tk), pr], m_ref, l_ref, acc_ref)
        return carry
    lax.fori_loop(0, nk, attn_body, 0)
    _flash_finish(y_ref, l_ref, acc_ref, A_HEADS)


def _tile_tables(positions, tq, tk):
    b, s = positions.shape
    pq = positions.reshape(b, s // tq, tq)
    pk = positions.reshape(b, s // tk, tk)
    pc = positions.reshape(b, s // LANES, LANES)
    q_lo, q_hi = pq.min(-1), pq.max(-1)
    vis = (pk.min(-1) // CHUNK)[:, None, :] <= (q_hi // CHUNK)[:, :, None]
    last = jnp.max(jnp.where(vis, jnp.arange(s // tk, dtype=I32)[None, None, :] + 1, 0), axis=-1)
    flat = lambda a: a.reshape(-1).astype(I32)
    return flat(last), flat(q_lo), flat(q_hi), flat(pc.min(-1)), flat(pc.max(-1))


def _attn_a(qa, qi, wi, ka, va, ki, positions, rel_bias, tq=256, tk=512):
    b, s = positions.shape
    n = b * s
    nq, nkt = s // tq, s // tk
    topk = min(TOPK_MAX, s // 4)
    tables = _tile_tables(positions, tq, tk)
    pos_col = positions.reshape(n, 1)
    pos_row = positions.reshape(b, nkt, 1, tk)
    qrow = lambda w: pl.BlockSpec((tq, w), lambda bi, i, *_: (bi * nq + i, 0))
    kfull = lambda w: pl.BlockSpec((s, w), lambda bi, i, *_: (bi, 0))
    kern = functools.partial(_attn_a_kernel, topk=topk, tq=tq, tk=tk, nq=nq, nkt=nkt)
    grid_spec = pltpu.PrefetchScalarGridSpec(
        num_scalar_prefetch=5,
        grid=(b, nq),
        in_specs=[pl.BlockSpec(memory_space=pltpu.SMEM),
                  qrow(D_A), qrow(D_A), qrow(LANES), kfull(D_A), kfull(D_A), kfull(LANES),
                  pl.BlockSpec((tq, 1), lambda bi, i, *_: (bi * nq + i, 0)),
                  pl.BlockSpec((None, nkt, 1, tk), lambda bi, i, *_: (bi, 0, 0, 0))],
        out_specs=qrow(D_A),
        scratch_shapes=[pltpu.VMEM((nkt, tq, tk), I32), pltpu.VMEM((nkt, tq, tk), F32),
                        pltpu.VMEM((A_HEADS, tq, tk), F32), pltpu.VMEM((tq, LANES), I32),
                        pltpu.VMEM((IDX_HEADS, tq, LANES), F32),
                        pltpu.VMEM((A_HEADS, tq, LANES), F32), pltpu.VMEM((A_HEADS, tq, LANES), F32),
                        pltpu.VMEM((A_HEADS, tq, LANES), F32)])
    return pl.pallas_call(
        kern, grid_spec=grid_spec,
        out_shape=jax.ShapeDtypeStruct((n, D_A), BF16),
        compiler_params=_cparams(("arbitrary", "arbitrary")),
        name="attn_indexer",
    )(*tables, rel_bias.reshape(-1), qa, qi, wi, ka, va, ki, pos_col, pos_row)


def _attn_b_kernel(nk_ref, qb_ref, kb_ref, vb_ref, pq_ref, pk_ref, y_ref, m_ref, l_ref, acc_ref, *, tq, tk, nq):
    b = pl.program_id(0)
    nk = nk_ref[b * nq + pl.program_id(1)]
    qchunk = jnp.right_shift(pq_ref[...], int(math.log2(CHUNK)))
    _flash_init(m_ref, l_ref, acc_ref)

    def body(kt, carry):
        start = pl.multiple_of(kt * tk, tk)
        kchunk = jnp.right_shift(pk_ref[kt], int(math.log2(CHUNK)))
        am = jnp.where(kchunk <= qchunk, 0.0, NEG_INF)
        for hd in range(B_HEADS):
            hs = slice(hd * HB, (hd + 1) * HB)
            pr = slice((hd // 2) * LANES, (hd // 2 + 1) * LANES)
            s = _dot_t(qb_ref[:, hs], kb_ref[pl.ds(start, tk), hs]) + am
            _flash_step(hd, s, vb_ref[pl.ds(start, tk), pr], m_ref, l_ref, acc_ref)
        return carry
    lax.fori_loop(0, nk, body, 0)
    _flash_finish(y_ref, l_ref, acc_ref, B_HEADS)


def _attn_b(qb, kb, vb, positions, tq=256, tk=512):
    b, s = positions.shape
    n = b * s
    nq, nkt = s // tq, s // tk
    nk = _tile_tables(positions, tq, tk)[0]
    qrow = lambda w: pl.BlockSpec((tq, w), lambda bi, i, *_: (bi * nq + i, 0))
    kfull = lambda w: pl.BlockSpec((s, w), lambda bi, i, *_: (bi, 0))
    grid_spec = pltpu.PrefetchScalarGridSpec(
        num_scalar_prefetch=1,
        grid=(b, nq),
        in_specs=[qrow(B_HEADS * HB), kfull(B_HEADS * HB), kfull(D_B),
                  pl.BlockSpec((tq, 1), lambda bi, i, *_: (bi * nq + i, 0)),
                  pl.BlockSpec((None, nkt, 1, tk), lambda bi, i, *_: (bi, 0, 0, 0))],
        out_specs=qrow(D_B),
        scratch_shapes=[pltpu.VMEM((B_HEADS, tq, LANES), F32)] * 3)
    return pl.pallas_call(
        functools.partial(_attn_b_kernel, tq=tq, tk=tk, nq=nq), grid_spec=grid_spec,
        out_shape=jax.ShapeDtypeStruct((n, D_B), BF16),
        compiler_params=_cparams(("arbitrary", "arbitrary")),
        name="attn_latent",
    )(nk, qb, kb, vb, positions.reshape(n, 1), positions.reshape(b, nkt, 1, tk))


def _out_router_kernel(x_ref, ya_ref, yb_ref, wo_ref, gm_ref, g_ref, sc_ref, sh_ref, wr_ref, br_ref,
                       x1_ref, h2_ref, e_ref, gate_ref, *, n_experts):
    mix = (jnp.dot(ya_ref[...], wo_ref[:D_A, :], preferred_element_type=F32)
           + jnp.dot(yb_ref[...], wo_ref[D_A:, :], preferred_element_type=F32))
    x1 = x_ref[...] + gm_ref[0] * mix
    x1_ref[...] = x1
    h2 = _rms(x1, g_ref[...]) * (1.0 + sc_ref[0]) + sh_ref[0]
    h2_ref[...] = h2
    logits = jnp.dot(h2, wr_ref[...], preferred_element_type=F32, precision=lax.Precision.HIGHEST) + br_ref[...]
    lane = lax.broadcasted_iota(I32, logits.shape, 1)
    cur = jnp.where(lane < n_experts, logits, NEG_INF)
    e_out = jnp.zeros(logits.shape, I32)
    g_out = jnp.zeros(logits.shape, F32)
    top = None
    for k in range(TOP_K):
        m = jnp.max(cur, axis=1, keepdims=True)
        idx = jnp.min(jnp.where(cur == m, lane, LANES), axis=1, keepdims=True)
        top = m if top is None else top
        e_out = jnp.where(lane == k, idx, e_out)
        g_out = jnp.where(lane == k, jnp.exp(m - top), g_out)
        cur = jnp.where(lane == idx, NEG_INF, cur)
    e_ref[...] = e_out
    gate_ref[...] = g_out / jnp.sum(g_out, axis=1, keepdims=True)


def _out_router(x2, ya, yb, wo, gm, g, sc, sh, wr, br, seq, tm=256):
    n, d = x2.shape
    tpb = seq // tm
    n_experts = wr.shape[1]
    wr_p = jnp.pad(wr, ((0, 0), (0, LANES - n_experts)))
    br_p = jnp.pad(br, (0, LANES - n_experts)).reshape(1, LANES)
    row = lambda w: pl.BlockSpec((tm, w), lambda i: (i, 0))
    full = lambda a: pl.BlockSpec(a.shape, lambda i: (0,) * a.ndim)
    per_b = pl.BlockSpec((1, 1, d), lambda i: (i // tpb, 0, 0))
    return pl.pallas_call(
        functools.partial(_out_router_kernel, n_experts=n_experts),
        grid=(n // tm,),
        in_specs=[row(d), row(D_A), row(D_B), full(wo), per_b, full(g), per_b, per_b, full(wr_p), full(br_p)],
        out_specs=[row(d), row(d), row(LANES), row(LANES)],
        out_shape=[jax.ShapeDtypeStruct((n, d), F32), jax.ShapeDtypeStruct((n, d), F32),
                   jax.ShapeDtypeStruct((n, LANES), I32), jax.ShapeDtypeStruct((n, LANES), F32)],
        compiler_params=_cparams(("arbitrary",)),
        name="out_proj_router",
    )(x2, ya, yb, wo, gm, g, sc, sh, wr_p, br_p)


def _row_copy(src_hbm, row, dst_ref, slot, sem):
    return pltpu.make_async_copy(src_hbm.at[pl.ds(row, 1), :], dst_ref.at[pl.ds(slot, 1), :], sem)


def _moe_gather_kernel(tok_ref, h_hbm, o_ref, sem):
    base = pl.program_id(0) * MOE_BLOCK

    def issue(r, carry):
        _row_copy(h_hbm, tok_ref[base + r], o_ref, r, sem).start()
        return carry
    lax.fori_loop(0, MOE_BLOCK, issue, 0, unroll=8)
    pltpu.make_async_copy(h_hbm.at[pl.ds(0, MOE_BLOCK), :], o_ref, sem).wait()


def _moe_gather(buf_tok, h2):
    cap = buf_tok.shape[0]
    d = h2.shape[1]
    grid_spec = pltpu.PrefetchScalarGridSpec(
        num_scalar_prefetch=1, grid=(cap // MOE_BLOCK,),
        in_specs=[pl.BlockSpec(memory_space=pl.ANY)],
        out_specs=pl.BlockSpec((MOE_BLOCK, d), lambda i, *_: (i, 0)),
        scratch_shapes=[pltpu.SemaphoreType.DMA])
    return pl.pallas_call(
        _moe_gather_kernel, grid_spec=grid_spec,
        out_shape=jax.ShapeDtypeStruct((cap, d), h2.dtype),
        compiler_params=_cparams(("arbitrary",)),
        name="moe_gather",
    )(buf_tok, h2)


def _moe_expert_kernel(be_ref, nu_ref, xs_ref, w1_ref, b1_ref, w2_ref, b2_ref, y_ref, w1b_ref, w2b_ref, *, d_exp):
    i = pl.program_id(0)
    live = i < nu_ref[0]
    changed = jnp.logical_or(i == 0, be_ref[i] != be_ref[jnp.maximum(i - 1, 0)])

    @pl.when(live & changed)
    def _():
        w1b_ref[...] = w1_ref[0].astype(BF16)
        w2b_ref[...] = w2_ref[0].astype(BF16)

    @pl.when(live)
    def _():
        hid = jnp.dot(xs_ref[...].astype(BF16), w1b_ref[...], preferred_element_type=F32) + b1_ref[0]
        glu = jnp.minimum(hid[:, :d_exp], SWIGLU_LIMIT)
        lin = jnp.clip(hid[:, d_exp:], -SWIGLU_LIMIT, SWIGLU_LIMIT)
        act = glu * jax.nn.sigmoid(SWIGLU_ALPHA * glu) * (lin + 1.0)
        y_ref[...] = jnp.dot(act.astype(BF16), w2b_ref[...], preferred_element_type=F32) + b2_ref[0]

    @pl.when(jnp.logical_not(live))
    def _():
        y_ref[...] = jnp.zeros(y_ref.shape, y_ref.dtype)


def _moe_experts(block_e, n_used, xs, w1, b1, w2, b2):
    cap, d = xs.shape
    d2 = w1.shape[-1]
    d_exp = d2 // 2
    n_exp = w1.shape[0] * w1.shape[1]
    w1 = w1.reshape(n_exp, d, d2)
    w2 = w2.reshape(n_exp, d_exp, d)
    grid_spec = pltpu.PrefetchScalarGridSpec(
        num_scalar_prefetch=2, grid=(cap // MOE_BLOCK,),
        in_specs=[pl.BlockSpec((MOE_BLOCK, d), lambda i, be, nu: (i, 0)),
                  pl.BlockSpec((1, d, d2), lambda i, be, nu: (be[i], 0, 0)),
                  pl.BlockSpec((1, 1, d2), lambda i, be, nu: (be[i], 0, 0)),
                  pl.BlockSpec((1, d_exp, d), lambda i, be, nu: (be[i], 0, 0)),
                  pl.BlockSpec((1, 1, d), lambda i, be, nu: (be[i], 0, 0))],
        out_specs=pl.BlockSpec((MOE_BLOCK, d), lambda i, be, nu: (i, 0)),
        scratch_shapes=[pltpu.VMEM((d, d2), BF16), pltpu.VMEM((d_exp, d), BF16)])
    return pl.pallas_call(
        functools.partial(_moe_expert_kernel, d_exp=d_exp), grid_spec=grid_spec,
        out_shape=jax.ShapeDtypeStruct((cap, d), F32),
        compiler_params=_cparams(("arbitrary",)),
        name="moe_experts",
    )(block_e, n_used, xs, w1, b1.reshape(n_exp, 1, d2), w2, b2.reshape(n_exp, 1, d))


def _moe_combine_kernel(dest_ref, ys_hbm, x1_ref, gate_ref, gf_ref, gfin_ref, o_ref, rows_ref, sem, *, tm, final):
    base = pl.program_id(0) * tm * TOP_K

    def issue(t, carry):
        for k in range(TOP_K):
            _row_copy(ys_hbm, dest_ref[base + t * TOP_K + k], rows_ref.at[k], t, sem).start()
        return carry
    lax.fori_loop(0, tm, issue, 0)
    for k in range(TOP_K):
        pltpu.make_async_copy(ys_hbm.at[pl.ds(0, tm), :], rows_ref.at[k], sem).wait()
    moe = gate_ref[:, 0:1] * rows_ref[0]
    for k in range(1, TOP_K):
        moe = moe + gate_ref[:, k:k + 1] * rows_ref[k]
    out = x1_ref[...] + gf_ref[0] * moe
    if final:
        out = _rms(out, gfin_ref[...])
    o_ref[...] = out


def _moe_combine(dest, ys, x1, gate, gf, g_final, seq, final, tm=256):
    n, d = x1.shape
    tpb = seq // tm
    grid_spec = pltpu.PrefetchScalarGridSpec(
        num_scalar_prefetch=1, grid=(n // tm,),
        in_specs=[pl.BlockSpec(memory_space=pl.ANY),
                  pl.BlockSpec((tm, d), lambda i, *_: (i, 0)),
                  pl.BlockSpec((tm, LANES), lambda i, *_: (i, 0)),
                  pl.BlockSpec((1, 1, d), lambda i, *_: (i // tpb, 0, 0)),
                  pl.BlockSpec((1, d), lambda i, *_: (0, 0))],
        out_specs=pl.BlockSpec((tm, d), lambda i, *_: (i, 0)),
        scratch_shapes=[pltpu.VMEM((TOP_K, tm, d), F32), pltpu.SemaphoreType.DMA])
    return pl.pallas_call(
        functools.partial(_moe_combine_kernel, tm=tm, final=final), grid_spec=grid_spec,
        out_shape=jax.ShapeDtypeStruct((n, d), F32),
        compiler_params=_cparams(("arbitrary",)),
        name="moe_combine",
    )(dest, ys, x1, gate, gf, g_final)


def _route_tables(top_e, n_experts):
    n = top_e.shape[0]
    nk = n * TOP_K
    n_blocks = -(-nk // MOE_BLOCK) + n_experts
    cap = n_blocks * MOE_BLOCK
    flat_e = top_e.reshape(nk)
    onehot = (flat_e[:, None] == jnp.arange(n_experts, dtype=I32)[None, :]).astype(I32)
    csum = jnp.cumsum(onehot, axis=0)
    rank = jnp.sum(csum * onehot, axis=1) - 1
    counts = csum[-1]
    padded = (counts + MOE_BLOCK - 1) // MOE_BLOCK * MOE_BLOCK
    pend = jnp.cumsum(padded)
    pstart = pend - padded
    dest = (pstart[flat_e] + rank).astype(I32)
    flat_tok = jnp.repeat(jnp.arange(n, dtype=I32), TOP_K)
    buf_tok = jnp.zeros((cap,), I32).at[dest].set(flat_tok)
    block_e = jnp.minimum(jnp.searchsorted(pend, jnp.arange(n_blocks, dtype=I32) * MOE_BLOCK, side='right'),
                          n_experts - 1).astype(I32)
    n_used = (pend[-1:] // MOE_BLOCK).astype(I32)
    return dest, buf_tok, block_e, n_used


def kernel(x, c, positions, rel_bias, norm_mix, w_ada, b_ada, w_in, q_norm, w_uq, kv_norm, w_ukv, w_out, norm_ffn, w_router, b_router, w1, b1, w2, b2, norm_final):
    bsz, seq, d = x.shape
    depth = w_ada.shape[0]
    n = bsz * seq
    n_experts = w_router.shape[-1]
    mod = _ada_mod(c, w_ada, b_ada)
    ctab, stab = _rope_tables(positions)
    row = lambda v: v.reshape(1, -1)
    x2 = x.reshape(n, d)
    for l in range(depth):
        sh_m, sc_m, g_m, sh_f, sc_f, g_f = [mod[l, :, i * d:(i + 1) * d].reshape(bsz, 1, d) for i in range(6)]
        wuq, wuqs = _pack_uq(w_uq[l])
        wukk, wukv = _pack_ukv(w_ukv[l])
        qa, ka, va, qi, ki, wi, qb, kb, vb = _proj(
            x2, row(norm_mix[l]), sc_m, sh_m, _pack_in_weights(w_in[l]), row(q_norm[l]), row(kv_norm[l]),
            wuq, wuqs, wukk, wukv, ctab, stab, seq)
        ya = _attn_a(qa, qi, wi, ka, va, ki, positions, rel_bias)
        yb = _attn_b(qb, kb, vb, positions)
        x1, h2, top_e, gate = _out_router(x2, ya, yb, w_out[l].astype(BF16), g_m, row(norm_ffn[l]), sc_f, sh_f,
                                          w_router[l], b_router[l], seq)
        dest, buf_tok, block_e, n_used = _route_tables(top_e[:, :TOP_K], n_experts)
        xs = _moe_gather(buf_tok, h2)
        ys = _moe_experts(block_e + l * n_experts, n_used, xs, w1, b1, w2, b2)
        x2 = _moe_combine(dest, ys, x1, gate, g_f, row(norm_final), seq, final=(l == depth - 1))
    return x2.reshape(bsz, seq, d)
```

```python
import functools
import math

import numpy as np
import jax
import jax.numpy as jnp
from jax import lax
from jax.experimental import pallas as pl
from jax.experimental.pallas import tpu as pltpu

CHUNK = 64
EPS = 1e-6
A_HEADS = 8
A_HEAD_DIM = 64
IDX_HEADS = 8
IDX_DIM = 64
TOPK_MAX = 256
B_HEADS = 8
Q_LORA = 256
KV_LORA = 128
QK_NOPE = 64
QK_ROPE = 32
V_DIM = 64
ROPE_THETA = 10000.0
N_BUCKETS = 32
MAX_DISTANCE = 128
TOP_K = 4
SWIGLU_LIMIT = 7.0
SWIGLU_ALPHA = 1.702
MOE_BLOCK = 256

LANES = 128
SUBLANES = 8
VMEM_LIMIT = 56 * 1024 * 1024
INT_MIN = -2 ** 31
NEG_INF = float("-inf")
BIAS_ROWS = 32

F32 = jnp.float32
BF16 = jnp.bfloat16
I32 = jnp.int32

_T5_EXACT = (N_BUCKETS // 2) // 2
_T5_THRESH = tuple(
    int(math.ceil(_T5_EXACT * (MAX_DISTANCE / _T5_EXACT) ** (k / (N_BUCKETS // 2 - _T5_EXACT)) - 1e-9))
    for k in range(1, N_BUCKETS // 2 - _T5_EXACT))
_T5_FAR = _T5_THRESH[-1]


def _cparams(sem, vmem=VMEM_LIMIT):
    return pltpu.CompilerParams(dimension_semantics=sem, vmem_limit_bytes=vmem)


def _dot_t(a, b):
    return lax.dot_general(a, b, (((1,), (1,)), ((), ())), preferred_element_type=F32)


def _ada_kernel(c_ref, w_ref, b_ref, o_ref):
    c = c_ref[...]
    cond = c * jax.nn.sigmoid(c)
    o_ref[0] = jnp.dot(cond, w_ref[0], preferred_element_type=F32,
                       precision=lax.Precision.HIGHEST) + b_ref[0]


def _ada_mod(c, w_ada, b_ada, tn=512):
    depth, d, n6 = w_ada.shape
    b = c.shape[0]
    return pl.pallas_call(
        _ada_kernel,
        grid=(depth, n6 // tn),
        in_specs=[pl.BlockSpec((b, d), lambda l, j: (0, 0)),
                  pl.BlockSpec((1, d, tn), lambda l, j: (l, 0, j)),
                  pl.BlockSpec((1, 1, tn), lambda l, j: (l, 0, j))],
        out_specs=pl.BlockSpec((1, b, tn), lambda l, j: (l, 0, j)),
        out_shape=jax.ShapeDtypeStruct((depth, b, n6), F32),
        compiler_params=_cparams(("arbitrary", "arbitrary")),
        name="ada_mod",
    )(c, w_ada, b_ada.reshape(depth, 1, n6))


D_A = A_HEADS * A_HEAD_DIM
D_B = B_HEADS * V_DIM
HB = 128
_C_QA, _C_KA, _C_VA, _C_QI = 0, D_A, 2 * D_A, 3 * D_A
_C_KI = 4 * D_A
_C_WI = _C_KI + LANES
_C_CQ = _C_WI + LANES
_C_CKV = _C_CQ + Q_LORA
_C_KR = _C_CKV + KV_LORA
_C_KRS = _C_KR + LANES
_C_END = _C_KRS + LANES


def _pack_in_weights(w_in):
    d = w_in.shape[0]
    offs = np.cumsum((0, D_A, D_A, D_A, IDX_HEADS * IDX_DIM, IDX_DIM, IDX_HEADS, Q_LORA, KV_LORA, QK_ROPE))
    seg = [w_in[:, offs[i]:offs[i + 1]] for i in range(9)]
    q_a, k_a, v_a, q_i, k_i, w_i, c_q, c_kv, k_r = seg
    z = lambda n: jnp.zeros((d, n), w_in.dtype)
    half = QK_ROPE // 2
    k_rs = jnp.concatenate([k_r[:, half:], k_r[:, :half]], axis=1)
    cols = [q_a * (A_HEAD_DIM ** -0.5), k_a, v_a, q_i, k_i, k_i, w_i, z(LANES - IDX_HEADS), c_q, c_kv,
            z(QK_NOPE), k_r, z(HB - QK_NOPE - QK_ROPE), z(QK_NOPE), k_rs, z(HB - QK_NOPE - QK_ROPE)]
    return jnp.concatenate(cols, axis=1).astype(BF16)


def _pack_uq(w_uq):
    r = w_uq.shape[0]
    w = w_uq.reshape(r, B_HEADS, QK_NOPE + QK_ROPE)
    nope, rope = w[..., :QK_NOPE], w[..., QK_NOPE:]
    half = QK_ROPE // 2
    z = jnp.zeros((r, B_HEADS, HB - QK_NOPE - QK_ROPE), w.dtype)
    main = jnp.concatenate([nope, rope, z], axis=-1).reshape(r, B_HEADS * HB)
    swap = jnp.concatenate([jnp.zeros_like(nope), rope[..., half:], rope[..., :half], z], axis=-1)
    return main.astype(BF16), swap.reshape(r, B_HEADS * HB).astype(BF16)


def _pack_ukv(w_ukv):
    r = w_ukv.shape[0]
    w = w_ukv.reshape(r, B_HEADS, QK_NOPE + V_DIM)
    k = jnp.concatenate([w[..., :QK_NOPE], jnp.zeros((r, B_HEADS, HB - QK_NOPE), w.dtype)], axis=-1)
    v = w[..., QK_NOPE:]
    return k.reshape(r, B_HEADS * HB).astype(BF16), v.reshape(r, B_HEADS * V_DIM).astype(BF16)


def _rope_tables(positions):
    half = QK_ROPE // 2
    inv = ROPE_THETA ** (-jnp.arange(half, dtype=F32) / half)
    ang = positions.astype(F32).reshape(-1, 1) * inv[None, :]
    cos, sin = jnp.cos(ang), jnp.sin(ang)
    n = ang.shape[0]
    one = jnp.ones((n, QK_NOPE), F32)
    z = lambda k: jnp.zeros((n, k), F32)
    ctab = jnp.concatenate([one, cos, cos, z(HB - QK_NOPE - QK_ROPE)], axis=1)
    stab = jnp.concatenate([z(QK_NOPE), -sin, sin, z(HB - QK_NOPE - QK_ROPE)], axis=1)
    return ctab, stab


def _rms(x, g):
    return x * lax.rsqrt(jnp.mean(x * x, axis=-1, keepdims=True) + EPS) * g


def _proj_kernel(x_ref, g_ref, sc_ref, sh_ref, wp_ref, qn_ref, kvn_ref, wuq_ref, wuqs_ref, wukk_ref, wukv_ref,
                 ct_ref, st_ref,
                 qa_ref, ka_ref, va_ref, qi_ref, ki_ref, wi_ref, qb_ref, kb_ref, vb_ref):
    x = x_ref[...]
    h = _rms(x, g_ref[...]) * (1.0 + sc_ref[0]) + sh_ref[0]
    hb = h.astype(BF16)
    seg = lambda a, b: jnp.dot(hb, wp_ref[:, a:b], preferred_element_type=F32)
    qa_ref[...] = seg(_C_QA, _C_KA).astype(BF16)
    ka_ref[...] = seg(_C_KA, _C_VA).astype(BF16)
    va_ref[...] = seg(_C_VA, _C_QI).astype(BF16)
    qi_ref[...] = seg(_C_QI, _C_KI).astype(BF16)
    ki_ref[...] = seg(_C_KI, _C_WI).astype(BF16)
    wi_ref[...] = seg(_C_WI, _C_CQ)
    ct = ct_ref[...]
    st = st_ref[...]
    cq = _rms(seg(_C_CQ, _C_CKV), qn_ref[...]).astype(BF16)
    scale = (QK_NOPE + QK_ROPE) ** -0.5
    for hd in range(B_HEADS):
        sl = slice(hd * HB, (hd + 1) * HB)
        q = jnp.dot(cq, wuq_ref[:, sl], preferred_element_type=F32)
        qs = jnp.dot(cq, wuqs_ref[:, sl], preferred_element_type=F32)
        qb_ref[:, sl] = ((q * ct + qs * st) * scale).astype(BF16)
    ckv = _rms(seg(_C_CKV, _C_KR), kvn_ref[...]).astype(BF16)
    kr = seg(_C_KR, _C_KRS) * ct + seg(_C_KRS, _C_END) * st
    for hd in range(B_HEADS):
        sl = slice(hd * HB, (hd + 1) * HB)
        kb_ref[:, sl] = (jnp.dot(ckv, wukk_ref[:, sl], preferred_element_type=F32) + kr).astype(BF16)
    vb_ref[...] = jnp.dot(ckv, wukv_ref[...], preferred_element_type=F32).astype(BF16)


def _proj(x2, g, sc, sh, wp, qn, kvn, wuq, wuqs, wukk, wukv, ctab, stab, seq, tm=256):
    n, d = x2.shape
    tpb = seq // tm
    row = lambda w: pl.BlockSpec((tm, w), lambda i: (i, 0))
    full = lambda a: pl.BlockSpec(a.shape, lambda i: (0,) * a.ndim)
    per_b = pl.BlockSpec((1, 1, d), lambda i: (i // tpb, 0, 0))
    outs = [(D_A, BF16), (D_A, BF16), (D_A, BF16), (D_A, BF16), (LANES, BF16), (LANES, F32),
            (B_HEADS * HB, BF16), (B_HEADS * HB, BF16), (D_B, BF16)]
    return pl.pallas_call(
        _proj_kernel,
        grid=(n // tm,),
        in_specs=[row(d), full(g), per_b, per_b, full(wp), full(qn), full(kvn), full(wuq), full(wuqs),
                  full(wukk), full(wukv), row(HB), row(HB)],
        out_specs=[row(w) for w, _ in outs],
        out_shape=[jax.ShapeDtypeStruct((n, w), dt) for w, dt in outs],
        compiler_params=_cparams(("arbitrary",)),
        name="in_proj",
    )(x2, g, sc, sh, wp, qn, kvn, wuq, wuqs, wukk, wukv, ctab, stab)


def _pair_mask(x_pair, odd):
    lane = lax.broadcasted_iota(I32, x_pair.shape, 1)
    keep = (lane >= A_HEAD_DIM) if odd else (lane < A_HEAD_DIM)
    return jnp.where(keep, x_pair, jnp.zeros_like(x_pair))


def _flash_step(hd, s, v_pair, m_ref, l_ref, acc_ref):
    m_prev = m_ref[hd]
    m_new = jnp.maximum(m_prev, jnp.max(s, axis=1, keepdims=True))
    m_safe = jnp.where(m_new == NEG_INF, 0.0, m_new)
    alpha = jnp.exp(m_prev - m_safe)
    p = jnp.exp(s - jnp.concatenate([m_safe] * (s.shape[1] // LANES), axis=1))
    l_ref[hd] = alpha * l_ref[hd] + jnp.sum(p, axis=1, keepdims=True)
    acc_ref[hd] = alpha * acc_ref[hd] + jnp.dot(p.astype(BF16), v_pair, preferred_element_type=F32)
    m_ref[hd] = m_new


def _flash_init(m_ref, l_ref, acc_ref):
    m_ref[...] = jnp.full(m_ref.shape, NEG_INF, F32)
    l_ref[...] = jnp.zeros(l_ref.shape, F32)
    acc_ref[...] = jnp.zeros(acc_ref.shape, F32)


def _flash_finish(y_ref, l_ref, acc_ref, n_heads):
    lane = lax.broadcasted_iota(I32, acc_ref.shape[1:], 1)
    for j in range(n_heads // 2):
        even = acc_ref[2 * j] / l_ref[2 * j]
        odd = acc_ref[2 * j + 1] / l_ref[2 * j + 1]
        y_ref[:, j * LANES:(j + 1) * LANES] = jnp.where(lane < V_DIM, even, odd).astype(y_ref.dtype)


def _t5_bias_heads(rel, bias_ref):
    nb = N_BUCKETS // 2
    n = jnp.abs(rel)
    bucket = jnp.minimum(n, _T5_EXACT)
    for t in _T5_THRESH:
        bucket = bucket + (n >= t).astype(I32)
    bucket = bucket + jnp.where(rel > 0, nb, 0)
    outs = [jnp.full(rel.shape, bias_ref[hd], F32) for hd in range(A_HEADS)]
    for j in range(1, N_BUCKETS):
        hit = bucket == j
        outs = [jnp.where(hit, bias_ref[j * A_HEADS + hd], outs[hd]) for hd in range(A_HEADS)]
    return outs


def _attn_a_kernel(nk_ref, qlo_ref, qhi_ref, klo_ref, khi_ref,
                   bias_ref, qa_ref, qi_ref, wi_ref, ka_ref, va_ref, ki_ref, pq_ref, pk_ref,
                   y_ref,
                   key_ref, am_ref, ex_ref, j_ref, wb_ref, m_ref, l_ref, acc_ref,
                   *, topk, tq, tk, nq, nkt):
    b = pl.program_id(0)
    qi_idx = pl.program_id(1)
    nk = nk_ref[b * nq + qi_idx]
    pq = pq_ref[...]
    qchunk = jnp.right_shift(pq, int(math.log2(CHUNK)))
    lane_tk = lax.broadcasted_iota(I32, (tq, tk), 1)

    wide = lambda v: jnp.concatenate([v] * (tk // LANES), axis=1)
    for hd in range(IDX_HEADS):
        wb_ref[hd] = jnp.broadcast_to(wi_ref[:, hd:hd + 1], (tq, LANES))

    def score_body(kt, carry):
        ks = ki_ref[pl.ds(pl.multiple_of(kt * tk, tk), tk), :]
        sc = jnp.zeros((tq, tk), F32)
        for hd in range(IDX_HEADS):
            pair = qi_ref[:, (hd // 2) * LANES:(hd // 2 + 1) * LANES]
            r = _dot_t(_pair_mask(pair, hd % 2), ks)
            sc = sc + wide(wb_ref[hd]) * jnp.maximum(r, 0.0)
        bits = pltpu.bitcast(sc, I32)
        key = bits ^ (jnp.right_shift(bits, 31) & jnp.int32(0x7FFFFFFF))
        kchunk = jnp.right_shift(pk_ref[kt], int(math.log2(CHUNK)))
        key_ref[kt] = jnp.where(kchunk <= qchunk, key, jnp.int32(INT_MIN))
        return carry
    lax.fori_loop(0, nk, score_body, 0)

    ones = jnp.ones((LANES, LANES), BF16)

    def count(pred):
        def body(kt, acc):
            hit = jnp.where(pred(key_ref[kt], kt), 1, 0)
            for c in range(tk // LANES):
                acc = acc + hit[:, c * LANES:(c + 1) * LANES]
            return acc
        part = lax.fori_loop(0, nk, body, jnp.zeros((tq, LANES), I32))
        return jnp.dot(part.astype(F32).astype(BF16), ones, preferred_element_type=F32)

    def bit_body(i, t):
        cand = t + jnp.left_shift(jnp.int32(1), 31 - i)
        cand_w = wide(cand)
        c = count(lambda k, kt: k >= cand_w)
        return jnp.where(c >= topk, cand, t)
    thr = lax.fori_loop(0, 32, bit_body, jnp.full((tq, LANES), INT_MIN, I32))
    thr_w = wide(thr)
    need = topk - count(lambda k, kt: k > thr_w)
    n_eq = count(lambda k, kt: k == thr_w)
    real = thr > jnp.int32(INT_MIN)
    j_ref[...] = jnp.where(real, jnp.int32(nkt * tk), jnp.int32(-1))
    tie = jnp.max(jnp.where(real & (n_eq > need), 1, 0))

    @pl.when(tie > 0)
    def _():
        def idx_body(i, p):
            cand = p + jnp.left_shift(jnp.int32(1), (nkt * tk).bit_length() - 2 - i)
            cand_w = wide(cand)
            c = count(lambda k, kt: (k == thr_w) & (lane_tk + kt * tk < cand_w))
            return jnp.where(c < need, cand, p)
        p = lax.fori_loop(0, (nkt * tk).bit_length() - 1, idx_body, jnp.zeros((tq, LANES), I32))
        j_ref[...] = jnp.where(real & (n_eq > need), p, j_ref[...])

    jsel_w = wide(j_ref[...])

    def mask_body(kt, carry):
        k = key_ref[kt]
        sel = (k > thr_w) | ((k == thr_w) & (lane_tk + kt * tk <= jsel_w))
        am_ref[kt] = jnp.where(sel, 0.0, NEG_INF)
        return carry
    lax.fori_loop(0, nk, mask_body, 0)

    _flash_init(m_ref, l_ref, acc_ref)
    q_lo = qlo_ref[b * nq + qi_idx]
    q_hi = qhi_ref[b * nq + qi_idx]

    def bias_chunk(kt, c):
        cs = slice(c * LANES, (c + 1) * LANES)
        chunk = b * (nkt * tk // LANES) + kt * (tk // LANES) + c
        far_past = khi_ref[chunk] - q_lo <= -_T5_FAR
        far_future = klo_ref[chunk] - q_hi >= _T5_FAR

        @pl.when(far_past)
        def _():
            for hd in range(A_HEADS):
                ex_ref[hd, :, cs] = am_ref[kt, :, cs] + bias_ref[(N_BUCKETS // 2 - 1) * A_HEADS + hd]

        @pl.when(far_future)
        def _():
            for hd in range(A_HEADS):
                ex_ref[hd, :, cs] = am_ref[kt, :, cs] + bias_ref[(N_BUCKETS - 1) * A_HEADS + hd]

        @pl.when(jnp.logical_not(far_past | far_future))
        def _():
            pkc = pk_ref[kt][:, cs]

            def rows_body(r, carry):
                rows = pl.ds(pl.multiple_of(r * BIAS_ROWS, BIAS_ROWS), BIAS_ROWS)
                am = am_ref[kt, rows, cs]
                for hd, bias in enumerate(_t5_bias_heads(pkc - pq_ref[rows, :], bias_ref)):
                    ex_ref[hd, rows, cs] = am + bias
                return carry
            lax.fori_loop(0, tq // BIAS_ROWS, rows_body, 0)

    def attn_body(kt, carry):
        start = pl.multiple_of(kt * tk, tk)
        for c in range(tk // LANES):
            bias_chunk(kt, c)

        for hd in range(A_HEADS):
            pr = slice((hd // 2) * LANES, (hd // 2 + 1) * LANES)
            qm = _pair_mask(qa_ref[:, pr], hd % 2)
            s = _dot_t(qm, ka_ref[pl.ds(start, tk), pr]) + ex_ref[hd]
            _flash_step(hd, s, va_ref[pl.ds(start, tk), pr], m_ref, l_ref, acc_ref)
        return carry
    lax.fori_loop(0, nk, attn_body, 0)
    _flash_finish(y_ref, l_ref, acc_ref, A_HEADS)


def _tile_tables(positions, tq, tk):
    b, s = positions.shape
    pq = positions.reshape(b, s // tq, tq)
    pk = positions.reshape(b, s // tk, tk)
    pc = positions.reshape(b, s // LANES, LANES)
    q_lo, q_hi = pq.min(-1), pq.max(-1)
    vis = (pk.min(-1) // CHUNK)[:, None, :] <= (q_hi // CHUNK)[:, :, None]
    last = jnp.max(jnp.where(vis, jnp.arange(s // tk, dtype=I32)[None, None, :] + 1, 0), axis=-1)
    flat = lambda a: a.reshape(-1).astype(I32)
    return flat(last), flat(q_lo), flat(q_hi), flat(pc.min(-1)), flat(pc.max(-1))


def _attn_a(qa, qi, wi, ka, va, ki, positions, rel_bias, tq=256, tk=512):
    b, s = positions.shape
    n = b * s
    nq, nkt = s // tq, s // tk
    topk = min(TOPK_MAX, s // 4)
    tables = _tile_tables(positions, tq, tk)
    pos_col = positions.reshape(n, 1)
    pos_row = positions.reshape(b, nkt, 1, tk)
    qrow = lambda w: pl.BlockSpec((tq, w), lambda bi, i, *_: (bi * nq + i, 0))
    kfull = lambda w: pl.BlockSpec((s, w), lambda bi, i, *_: (bi, 0))
    kern = functools.partial(_attn_a_kernel, topk=topk, tq=tq, tk=tk, nq=nq, nkt=nkt)
    grid_spec = pltpu.PrefetchScalarGridSpec(
        num_scalar_prefetch=5,
        grid=(b, nq),
        in_specs=[pl.BlockSpec(memory_space=pltpu.SMEM),
                  qrow(D_A), qrow(D_A), qrow(LANES), kfull(D_A), kfull(D_A), kfull(LANES),
                  pl.BlockSpec((tq, 1), lambda bi, i, *_: (bi * nq + i, 0)),
                  pl.BlockSpec((None, nkt, 1, tk), lambda bi, i, *_: (bi, 0, 0, 0))],
        out_specs=qrow(D_A),
        scratch_shapes=[pltpu.VMEM((nkt, tq, tk), I32), pltpu.VMEM((nkt, tq, tk), F32),
                        pltpu.VMEM((A_HEADS, tq, tk), F32), pltpu.VMEM((tq, LANES), I32),
                        pltpu.VMEM((IDX_HEADS, tq, LANES), F32),
                        pltpu.VMEM((A_HEADS, tq, LANES), F32), pltpu.VMEM((A_HEADS, tq, LANES), F32),
                        pltpu.VMEM((A_HEADS, tq, LANES), F32)])
    return pl.pallas_call(
        kern, grid_spec=grid_spec,
        out_shape=jax.ShapeDtypeStruct((n, D_A), BF16),
        compiler_params=_cparams(("arbitrary", "arbitrary")),
        name="attn_indexer",
    )(*tables, rel_bias.reshape(-1), qa, qi, wi, ka, va, ki, pos_col, pos_row)


def _attn_b_kernel(nk_ref, qb_ref, kb_ref, vb_ref, pq_ref, pk_ref, y_ref, m_ref, l_ref, acc_ref, *, tq, tk, nq):
    b = pl.program_id(0)
    nk = nk_ref[b * nq + pl.program_id(1)]
    qchunk = jnp.right_shift(pq_ref[...], int(math.log2(CHUNK)))
    _flash_init(m_ref, l_ref, acc_ref)

    def body(kt, carry):
        start = pl.multiple_of(kt * tk, tk)
        kchunk = jnp.right_shift(pk_ref[kt], int(math.log2(CHUNK)))
        am = jnp.where(kchunk <= qchunk, 0.0, NEG_INF)
        for hd in range(B_HEADS):
            hs = slice(hd * HB, (hd + 1) * HB)
            pr = slice((hd // 2) * LANES, (hd // 2 + 1) * LANES)
            s = _dot_t(qb_ref[:, hs], kb_ref[pl.ds(start, tk), hs]) + am
            _flash_step(hd, s, vb_ref[pl.ds(start, tk), pr], m_ref, l_ref, acc_ref)
        return carry
    lax.fori_loop(0, nk, body, 0)
    _flash_finish(y_ref, l_ref, acc_ref, B_HEADS)


def _attn_b(qb, kb, vb, positions, tq=256, tk=512):
    b, s = positions.shape
    n = b * s
    nq, nkt = s // tq, s // tk
    nk = _tile_tables(positions, tq, tk)[0]
    qrow = lambda w: pl.BlockSpec((tq, w), lambda bi, i, *_: (bi * nq + i, 0))
    kfull = lambda w: pl.BlockSpec((s, w), lambda bi, i, *_: (bi, 0))
    grid_spec = pltpu.PrefetchScalarGridSpec(
        num_scalar_prefetch=1,
        grid=(b, nq),
        in_specs=[qrow(B_HEADS * HB), kfull(B_HEADS * HB), kfull(D_B),
                  pl.BlockSpec((tq, 1), lambda bi, i, *_: (bi * nq + i, 0)),
                  pl.BlockSpec((None, nkt, 1, tk), lambda bi, i, *_: (bi, 0, 0, 0))],
        out_specs=qrow(D_B),
        scratch_shapes=[pltpu.VMEM((B_HEADS, tq, LANES), F32)] * 3)
    return pl.pallas_call(
        functools.partial(_attn_b_kernel, tq=tq, tk=tk, nq=nq), grid_spec=grid_spec,
        out_shape=jax.ShapeDtypeStruct((n, D_B), BF16),
        compiler_params=_cparams(("arbitrary", "arbitrary")),
        name="attn_latent",
    )(nk, qb, kb, vb, positions.reshape(n, 1), positions.reshape(b, nkt, 1, tk))


def _out_router_kernel(x_ref, ya_ref, yb_ref, wo_ref, gm_ref, g_ref, sc_ref, sh_ref, wr_ref, br_ref,
                       x1_ref, h2_ref, e_ref, gate_ref, rank_ref, cnt_ref, run_ref, *, n_experts):
    @pl.when(pl.program_id(0) == 0)
    def _():
        run_ref[...] = jnp.zeros(run_ref.shape, F32)

    mix = (jnp.dot(ya_ref[...], wo_ref[:D_A, :], preferred_element_type=F32)
           + jnp.dot(yb_ref[...], wo_ref[D_A:, :], preferred_element_type=F32))
    x1 = x_ref[...] + gm_ref[0] * mix
    x1_ref[...] = x1
    h2 = _rms(x1, g_ref[...]) * (1.0 + sc_ref[0]) + sh_ref[0]
    h2_ref[...] = h2
    logits = jnp.dot(h2, wr_ref[...], preferred_element_type=F32, precision=lax.Precision.HIGHEST) + br_ref[...]
    lane = lax.broadcasted_iota(I32, logits.shape, 1)
    cur = jnp.where(lane < n_experts, logits, NEG_INF)
    e_out = jnp.zeros(logits.shape, I32)
    g_out = jnp.zeros(logits.shape, F32)
    top = None
    picks = []
    for k in range(TOP_K):
        m = jnp.max(cur, axis=1, keepdims=True)
        idx = jnp.min(jnp.where(cur == m, lane, LANES), axis=1, keepdims=True)
        top = m if top is None else top
        e_out = jnp.where(lane == k, idx, e_out)
        g_out = jnp.where(lane == k, jnp.exp(m - top), g_out)
        picks.append(lane == idx)
        cur = jnp.where(picks[-1], NEG_INF, cur)
    e_ref[...] = e_out
    gate_ref[...] = g_out / jnp.sum(g_out, axis=1, keepdims=True)
    tm = logits.shape[0]
    chosen = jnp.where(picks[0] | picks[1] | picks[2] | picks[3], 1.0, 0.0)
    earlier = (lax.broadcasted_iota(I32, (tm, tm), 0) > lax.broadcasted_iota(I32, (tm, tm), 1))
    before = jnp.dot(jnp.where(earlier, 1.0, 0.0).astype(BF16), chosen.astype(BF16),
                     preferred_element_type=F32) + run_ref[0:1, :]
    r_out = jnp.zeros(logits.shape, I32)
    for k in range(TOP_K):
        rk = jnp.sum(jnp.where(picks[k], before, 0.0), axis=1, keepdims=True)
        r_out = jnp.where(lane == k, rk.astype(I32), r_out)
    rank_ref[...] = r_out
    run_ref[...] = run_ref[...] + jnp.sum(chosen, axis=0, keepdims=True)
    cnt_ref[...] = run_ref[...].astype(I32)


def _out_router(x2, ya, yb, wo, gm, g, sc, sh, wr, br, seq, tm=256):
    n, d = x2.shape
    tpb = seq // tm
    n_experts = wr.shape[1]
    wr_p = jnp.pad(wr, ((0, 0), (0, LANES - n_experts)))
    br_p = jnp.pad(br, (0, LANES - n_experts)).reshape(1, LANES)
    row = lambda w: pl.BlockSpec((tm, w), lambda i: (i, 0))
    full = lambda a: pl.BlockSpec(a.shape, lambda i: (0,) * a.ndim)
    per_b = pl.BlockSpec((1, 1, d), lambda i: (i // tpb, 0, 0))
    return pl.pallas_call(
        functools.partial(_out_router_kernel, n_experts=n_experts),
        grid=(n // tm,),
        in_specs=[row(d), row(D_A), row(D_B), full(wo), per_b, full(g), per_b, per_b, full(wr_p), full(br_p)],
        out_specs=[row(d), row(d), row(LANES), row(LANES), row(LANES),
                   pl.BlockSpec((SUBLANES, LANES), lambda i: (0, 0))],
        out_shape=[jax.ShapeDtypeStruct((n, d), F32), jax.ShapeDtypeStruct((n, d), F32),
                   jax.ShapeDtypeStruct((n, LANES), I32), jax.ShapeDtypeStruct((n, LANES), F32),
                   jax.ShapeDtypeStruct((n, LANES), I32), jax.ShapeDtypeStruct((SUBLANES, LANES), I32)],
        scratch_shapes=[pltpu.VMEM((SUBLANES, LANES), F32)],
        compiler_params=_cparams(("arbitrary",)),
        name="out_proj_router",
    )(x2, ya, yb, wo, gm, g, sc, sh, wr_p, br_p)


def _row_copy(src_hbm, row, dst_ref, slot, sem):
    return pltpu.make_async_copy(src_hbm.at[pl.ds(row, 1), :], dst_ref.at[pl.ds(slot, 1), :], sem)


def _moe_dispatch_kernel(dest_ref, pend_ref, h_hbm, xs_hbm, zero_ref, zsem, sem, *, tm, n_experts):
    i = pl.program_id(0)

    @pl.when(i == 0)
    def _():
        zero_ref[...] = jnp.zeros(zero_ref.shape, zero_ref.dtype)

        def tail(e):
            start = pl.multiple_of(pend_ref[e + 1] - MOE_BLOCK, MOE_BLOCK)
            return pltpu.make_async_copy(zero_ref, xs_hbm.at[pl.ds(start, MOE_BLOCK), :], zsem)

        def has_slots(e):
            return pend_ref[e + 1] > pend_ref[e]

        def unused(j):
            start = pl.multiple_of(pend_ref[n_experts] + j * MOE_BLOCK, MOE_BLOCK)
            return pltpu.make_async_copy(zero_ref, xs_hbm.at[pl.ds(start, MOE_BLOCK), :], zsem)

        def is_unused(j):
            return pend_ref[n_experts] + j * MOE_BLOCK < xs_hbm.shape[0]
        for e in range(n_experts):
            pl.when(has_slots(e))(lambda e=e: tail(e).start())
        for j in range(n_experts):
            pl.when(is_unused(j))(lambda j=j: unused(j).start())
        for e in range(n_experts):
            pl.when(has_slots(e))(lambda e=e: tail(e).wait())
        for j in range(n_experts):
            pl.when(is_unused(j))(lambda j=j: unused(j).wait())

    base = i * tm

    def issue(t, carry):
        for k in range(TOP_K):
            pltpu.make_async_copy(h_hbm.at[pl.ds(base + t, 1), :],
                                  xs_hbm.at[pl.ds(dest_ref[(base + t) * TOP_K + k], 1), :], sem).start()
        return carry
    lax.fori_loop(0, tm, issue, 0, unroll=2)
    for k in range(TOP_K):
        pltpu.make_async_copy(h_hbm.at[pl.ds(0, tm), :], xs_hbm.at[pl.ds(0, tm), :], sem).wait()


def _moe_dispatch(dest, pend0, h2, cap, tm=512):
    n, d = h2.shape
    n_experts = pend0.shape[0] - 1
    grid_spec = pltpu.PrefetchScalarGridSpec(
        num_scalar_prefetch=2, grid=(n // tm,),
        in_specs=[pl.BlockSpec(memory_space=pl.ANY)],
        out_specs=pl.BlockSpec(memory_space=pl.ANY),
        scratch_shapes=[pltpu.VMEM((MOE_BLOCK, d), h2.dtype), pltpu.SemaphoreType.DMA, pltpu.SemaphoreType.DMA])
    return pl.pallas_call(
        functools.partial(_moe_dispatch_kernel, tm=tm, n_experts=n_experts), grid_spec=grid_spec,
        out_shape=jax.ShapeDtypeStruct((cap, d), h2.dtype),
        compiler_params=_cparams(("arbitrary",)),
        name="moe_dispatch",
    )(dest, pend0, h2)


def _moe_expert_kernel(be_ref, nu_ref, xs_ref, w1_ref, b1_ref, w2_ref, b2_ref, y_ref, w1b_ref, w2b_ref, *, d_exp):
    i = pl.program_id(0)
    live = i < nu_ref[0]
    changed = jnp.logical_or(i == 0, be_ref[i] != be_ref[jnp.maximum(i - 1, 0)])

    @pl.when(live & changed)
    def _():
        w1b_ref[...] = w1_ref[0].astype(BF16)
        w2b_ref[...] = w2_ref[0].astype(BF16)

    @pl.when(live)
    def _():
        hid = jnp.dot(xs_ref[...].astype(BF16), w1b_ref[...], preferred_element_type=F32) + b1_ref[0]
        glu = jnp.minimum(hid[:, :d_exp], SWIGLU_LIMIT)
        lin = jnp.clip(hid[:, d_exp:], -SWIGLU_LIMIT, SWIGLU_LIMIT)
        act = glu * jax.nn.sigmoid(SWIGLU_ALPHA * glu) * (lin + 1.0)
        y_ref[...] = jnp.dot(act.astype(BF16), w2b_ref[...], preferred_element_type=F32) + b2_ref[0]

    @pl.when(jnp.logical_not(live))
    def _():
        y_ref[...] = jnp.zeros(y_ref.shape, y_ref.dtype)


def _moe_experts(block_e, n_used, xs, w1, b1, w2, b2):
    cap, d = xs.shape
    d2 = w1.shape[-1]
    d_exp = d2 // 2
    n_exp = w1.shape[0] * w1.shape[1]
    w1 = w1.reshape(n_exp, d, d2)
    w2 = w2.reshape(n_exp, d_exp, d)
    grid_spec = pltpu.PrefetchScalarGridSpec(
        num_scalar_prefetch=2, grid=(cap // MOE_BLOCK,),
        in_specs=[pl.BlockSpec((MOE_BLOCK, d), lambda i, be, nu: (jnp.minimum(i, nu[0] - 1), 0)),
                  pl.BlockSpec((1, d, d2), lambda i, be, nu: (be[i], 0, 0)),
                  pl.BlockSpec((1, 1, d2), lambda i, be, nu: (be[i], 0, 0)),
                  pl.BlockSpec((1, d_exp, d), lambda i, be, nu: (be[i], 0, 0)),
                  pl.BlockSpec((1, 1, d), lambda i, be, nu: (be[i], 0, 0))],
        out_specs=pl.BlockSpec((MOE_BLOCK, d), lambda i, be, nu: (i, 0)),
        scratch_shapes=[pltpu.VMEM((d, d2), BF16), pltpu.VMEM((d_exp, d), BF16)])
    return pl.pallas_call(
        functools.partial(_moe_expert_kernel, d_exp=d_exp), grid_spec=grid_spec,
        out_shape=jax.ShapeDtypeStruct((cap, d), F32),
        compiler_params=_cparams(("arbitrary",)),
        name="moe_experts",
    )(block_e, n_used, xs, w1, b1.reshape(n_exp, 1, d2), w2, b2.reshape(n_exp, 1, d))


def _moe_combine_kernel(dest_ref, ys_hbm, x1_ref, gate_ref, gf_ref, gfin_ref, o_ref, rows_ref, sem, *, tm, final):
    base = pl.program_id(0) * tm * TOP_K

    def issue(t, carry):
        for k in range(TOP_K):
            _row_copy(ys_hbm, dest_ref[base + t * TOP_K + k], rows_ref.at[k], t, sem).start()
        return carry
    lax.fori_loop(0, tm, issue, 0)
    for k in range(TOP_K):
        pltpu.make_async_copy(ys_hbm.at[pl.ds(0, tm), :], rows_ref.at[k], sem).wait()
    moe = gate_ref[:, 0:1] * rows_ref[0]
    for k in range(1, TOP_K):
        moe = moe + gate_ref[:, k:k + 1] * rows_ref[k]
    out = x1_ref[...] + gf_ref[0] * moe
    if final:
        out = _rms(out, gfin_ref[...])
    o_ref[...] = out


def _moe_combine(dest, ys, x1, gate, gf, g_final, seq, final, tm=256):
    n, d = x1.shape
    tpb = seq // tm
    grid_spec = pltpu.PrefetchScalarGridSpec(
        num_scalar_prefetch=1, grid=(n // tm,),
        in_specs=[pl.BlockSpec(memory_space=pl.ANY),
                  pl.BlockSpec((tm, d), lambda i, *_: (i, 0)),
                  pl.BlockSpec((tm, LANES), lambda i, *_: (i, 0)),
                  pl.BlockSpec((1, 1, d), lambda i, *_: (i // tpb, 0, 0)),
                  pl.BlockSpec((1, d), lambda i, *_: (0, 0))],
        out_specs=pl.BlockSpec((tm, d), lambda i, *_: (i, 0)),
        scratch_shapes=[pltpu.VMEM((TOP_K, tm, d), F32), pltpu.SemaphoreType.DMA])
    return pl.pallas_call(
        functools.partial(_moe_combine_kernel, tm=tm, final=final), grid_spec=grid_spec,
        out_shape=jax.ShapeDtypeStruct((n, d), F32),
        compiler_params=_cparams(("arbitrary",)),
        name="moe_combine",
    )(dest, ys, x1, gate, gf, g_final)


def _route_tables(top_e, rank, counts, n_blocks):
    n_experts = counts.shape[0]
    padded = (counts + MOE_BLOCK - 1) // MOE_BLOCK * MOE_BLOCK
    pend = jnp.cumsum(padded)
    pstart = pend - padded
    onehot = top_e[..., None] == jnp.arange(n_experts, dtype=I32)
    dest = (jnp.sum(jnp.where(onehot, pstart, 0), axis=-1) + rank).reshape(-1).astype(I32)
    block_e = jnp.minimum(jnp.sum(pend[None, :] <= jnp.arange(n_blocks, dtype=I32)[:, None] * MOE_BLOCK, axis=1),
                          n_experts - 1).astype(I32)
    n_used = (pend[-1:] // MOE_BLOCK).astype(I32)
    pend0 = jnp.concatenate([jnp.zeros((1,), I32), pend.astype(I32)])
    return dest, pend0, block_e, n_used


def kernel(x, c, positions, rel_bias, norm_mix, w_ada, b_ada, w_in, q_norm, w_uq, kv_norm, w_ukv, w_out, norm_ffn, w_router, b_router, w1, b1, w2, b2, norm_final):
    bsz, seq, d = x.shape
    depth = w_ada.shape[0]
    n = bsz * seq
    n_experts = w_router.shape[-1]
    mod = _ada_mod(c, w_ada, b_ada)
    ctab, stab = _rope_tables(positions)
    row = lambda v: v.reshape(1, -1)
    x2 = x.reshape(n, d)
    for l in range(depth):
        sh_m, sc_m, g_m, sh_f, sc_f, g_f = [mod[l, :, i * d:(i + 1) * d].reshape(bsz, 1, d) for i in range(6)]
        wuq, wuqs = _pack_uq(w_uq[l])
        wukk, wukv = _pack_ukv(w_ukv[l])
        qa, ka, va, qi, ki, wi, qb, kb, vb = _proj(
            x2, row(norm_mix[l]), sc_m, sh_m, _pack_in_weights(w_in[l]), row(q_norm[l]), row(kv_norm[l]),
            wuq, wuqs, wukk, wukv, ctab, stab, seq)
        ya = _attn_a(qa, qi, wi, ka, va, ki, positions, rel_bias)
        yb = _attn_b(qb, kb, vb, positions)
        x1, h2, top_e, gate, rank, counts = _out_router(x2, ya, yb, w_out[l].astype(BF16), g_m, row(norm_ffn[l]),
                                                        sc_f, sh_f, w_router[l], b_router[l], seq)
        n_blocks = -(-n * TOP_K // MOE_BLOCK) + n_experts
        dest, pend0, block_e, n_used = _route_tables(top_e[:, :TOP_K], rank[:, :TOP_K], counts[0, :n_experts],
                                                     n_blocks)
        xs = _moe_dispatch(dest, pend0, h2, n_blocks * MOE_BLOCK)
        ys = _moe_experts(block_e + l * n_experts, n_used, xs, w1, b1, w2, b2)
        x2 = _moe_combine(dest, ys, x1, gate, g_f, row(norm_final), seq, final=(l == depth - 1))
    return x2.reshape(bsz, seq, d)
```

```python
import functools
import math

import numpy as np
import jax
import jax.numpy as jnp
from jax import lax
from jax.experimental import pallas as pl
from jax.experimental.pallas import tpu as pltpu

CHUNK = 64
EPS = 1e-6
A_HEADS = 8
A_HEAD_DIM = 64
IDX_HEADS = 8
IDX_DIM = 64
TOPK_MAX = 256
B_HEADS = 8
Q_LORA = 256
KV_LORA = 128
QK_NOPE = 64
QK_ROPE = 32
V_DIM = 64
ROPE_THETA = 10000.0
N_BUCKETS = 32
MAX_DISTANCE = 128
TOP_K = 4
SWIGLU_LIMIT = 7.0
SWIGLU_ALPHA = 1.702
MOE_BLOCK = 256

LANES = 128
SUBLANES = 8
VMEM_LIMIT = 56 * 1024 * 1024
INT_MIN = -2 ** 31
NEG_INF = float("-inf")
BIAS_ROWS = 32
COUNT_ROWS = 128

F32 = jnp.float32
BF16 = jnp.bfloat16
I32 = jnp.int32

_T5_EXACT = (N_BUCKETS // 2) // 2
_T5_THRESH = tuple(
    int(math.ceil(_T5_EXACT * (MAX_DISTANCE / _T5_EXACT) ** (k / (N_BUCKETS // 2 - _T5_EXACT)) - 1e-9))
    for k in range(1, N_BUCKETS // 2 - _T5_EXACT))
_T5_FAR = _T5_THRESH[-1]


def _cparams(sem, vmem=VMEM_LIMIT):
    return pltpu.CompilerParams(dimension_semantics=sem, vmem_limit_bytes=vmem)


def _dot_t(a, b):
    return lax.dot_general(a, b, (((1,), (1,)), ((), ())), preferred_element_type=F32)


def _ada_kernel(c_ref, w_ref, b_ref, o_ref):
    c = c_ref[...]
    cond = c * jax.nn.sigmoid(c)
    o_ref[0] = jnp.dot(cond, w_ref[0], preferred_element_type=F32,
                       precision=lax.Precision.HIGHEST) + b_ref[0]


def _ada_mod(c, w_ada, b_ada, tn=512):
    depth, d, n6 = w_ada.shape
    b = c.shape[0]
    return pl.pallas_call(
        _ada_kernel,
        grid=(depth, n6 // tn),
        in_specs=[pl.BlockSpec((b, d), lambda l, j: (0, 0)),
                  pl.BlockSpec((1, d, tn), lambda l, j: (l, 0, j)),
                  pl.BlockSpec((1, 1, tn), lambda l, j: (l, 0, j))],
        out_specs=pl.BlockSpec((1, b, tn), lambda l, j: (l, 0, j)),
        out_shape=jax.ShapeDtypeStruct((depth, b, n6), F32),
        compiler_params=_cparams(("arbitrary", "arbitrary")),
        name="ada_mod",
    )(c, w_ada, b_ada.reshape(depth, 1, n6))


D_A = A_HEADS * A_HEAD_DIM
D_B = B_HEADS * V_DIM
HB = 128
_C_QA, _C_KA, _C_VA, _C_QI = 0, D_A, 2 * D_A, 3 * D_A
_C_KI = 4 * D_A
_C_WI = _C_KI + LANES
_C_CQ = _C_WI + LANES
_C_CKV = _C_CQ + Q_LORA
_C_KR = _C_CKV + KV_LORA
_C_KRS = _C_KR + LANES
_C_END = _C_KRS + LANES


def _pack_in_weights(w_in):
    d = w_in.shape[0]
    offs = np.cumsum((0, D_A, D_A, D_A, IDX_HEADS * IDX_DIM, IDX_DIM, IDX_HEADS, Q_LORA, KV_LORA, QK_ROPE))
    seg = [w_in[:, offs[i]:offs[i + 1]] for i in range(9)]
    q_a, k_a, v_a, q_i, k_i, w_i, c_q, c_kv, k_r = seg
    z = lambda n: jnp.zeros((d, n), w_in.dtype)
    half = QK_ROPE // 2
    k_rs = jnp.concatenate([k_r[:, half:], k_r[:, :half]], axis=1)
    cols = [q_a * (A_HEAD_DIM ** -0.5), k_a, v_a, q_i, k_i, k_i, w_i, z(LANES - IDX_HEADS), c_q, c_kv,
            z(QK_NOPE), k_r, z(HB - QK_NOPE - QK_ROPE), z(QK_NOPE), k_rs, z(HB - QK_NOPE - QK_ROPE)]
    return jnp.concatenate(cols, axis=1).astype(BF16)


def _pack_uq(w_uq):
    r = w_uq.shape[0]
    w = w_uq.reshape(r, B_HEADS, QK_NOPE + QK_ROPE)
    nope, rope = w[..., :QK_NOPE], w[..., QK_NOPE:]
    half = QK_ROPE // 2
    z = jnp.zeros((r, B_HEADS, HB - QK_NOPE - QK_ROPE), w.dtype)
    main = jnp.concatenate([nope, rope, z], axis=-1).reshape(r, B_HEADS * HB)
    swap = jnp.concatenate([jnp.zeros_like(nope), rope[..., half:], rope[..., :half], z], axis=-1)
    return main.astype(BF16), swap.reshape(r, B_HEADS * HB).astype(BF16)


def _pack_ukv(w_ukv):
    r = w_ukv.shape[0]
    w = w_ukv.reshape(r, B_HEADS, QK_NOPE + V_DIM)
    k = jnp.concatenate([w[..., :QK_NOPE], jnp.zeros((r, B_HEADS, HB - QK_NOPE), w.dtype)], axis=-1)
    v = w[..., QK_NOPE:]
    return k.reshape(r, B_HEADS * HB).astype(BF16), v.reshape(r, B_HEADS * V_DIM).astype(BF16)


def _rope_tables(positions):
    half = QK_ROPE // 2
    inv = ROPE_THETA ** (-jnp.arange(half, dtype=F32) / half)
    ang = positions.astype(F32).reshape(-1, 1) * inv[None, :]
    cos, sin = jnp.cos(ang), jnp.sin(ang)
    n = ang.shape[0]
    one = jnp.ones((n, QK_NOPE), F32)
    z = lambda k: jnp.zeros((n, k), F32)
    ctab = jnp.concatenate([one, cos, cos, z(HB - QK_NOPE - QK_ROPE)], axis=1)
    stab = jnp.concatenate([z(QK_NOPE), -sin, sin, z(HB - QK_NOPE - QK_ROPE)], axis=1)
    return ctab, stab


def _rms(x, g):
    return x * lax.rsqrt(jnp.mean(x * x, axis=-1, keepdims=True) + EPS) * g


def _proj_kernel(x_ref, g_ref, sc_ref, sh_ref, wp_ref, qn_ref, kvn_ref, wuq_ref, wuqs_ref, wukk_ref, wukv_ref,
                 ct_ref, st_ref,
                 qa_ref, ka_ref, va_ref, qi_ref, ki_ref, wi_ref, qb_ref, kb_ref, vb_ref):
    x = x_ref[...]
    h = _rms(x, g_ref[...]) * (1.0 + sc_ref[0]) + sh_ref[0]
    hb = h.astype(BF16)
    seg = lambda a, b: jnp.dot(hb, wp_ref[:, a:b], preferred_element_type=F32)
    qa_ref[...] = seg(_C_QA, _C_KA).astype(BF16)
    ka_ref[...] = seg(_C_KA, _C_VA).astype(BF16)
    va_ref[...] = seg(_C_VA, _C_QI).astype(BF16)
    qi_ref[...] = seg(_C_QI, _C_KI).astype(BF16)
    ki_ref[...] = seg(_C_KI, _C_WI).astype(BF16)
    wi_ref[...] = seg(_C_WI, _C_CQ)
    ct = ct_ref[...]
    st = st_ref[...]
    cq = _rms(seg(_C_CQ, _C_CKV), qn_ref[...]).astype(BF16)
    scale = (QK_NOPE + QK_ROPE) ** -0.5
    for hd in range(B_HEADS):
        sl = slice(hd * HB, (hd + 1) * HB)
        q = jnp.dot(cq, wuq_ref[:, sl], preferred_element_type=F32)
        qs = jnp.dot(cq, wuqs_ref[:, sl], preferred_element_type=F32)
        qb_ref[:, sl] = ((q * ct + qs * st) * scale).astype(BF16)
    ckv = _rms(seg(_C_CKV, _C_KR), kvn_ref[...]).astype(BF16)
    kr = seg(_C_KR, _C_KRS) * ct + seg(_C_KRS, _C_END) * st
    for hd in range(B_HEADS):
        sl = slice(hd * HB, (hd + 1) * HB)
        kb_ref[:, sl] = (jnp.dot(ckv, wukk_ref[:, sl], preferred_element_type=F32) + kr).astype(BF16)
    vb_ref[...] = jnp.dot(ckv, wukv_ref[...], preferred_element_type=F32).astype(BF16)


def _proj(x2, g, sc, sh, wp, qn, kvn, wuq, wuqs, wukk, wukv, ctab, stab, seq, tm=256):
    n, d = x2.shape
    tpb = seq // tm
    row = lambda w: pl.BlockSpec((tm, w), lambda i: (i, 0))
    full = lambda a: pl.BlockSpec(a.shape, lambda i: (0,) * a.ndim)
    per_b = pl.BlockSpec((1, 1, d), lambda i: (i // tpb, 0, 0))
    outs = [(D_A, BF16), (D_A, BF16), (D_A, BF16), (D_A, BF16), (LANES, BF16), (LANES, F32),
            (B_HEADS * HB, BF16), (B_HEADS * HB, BF16), (D_B, BF16)]
    return pl.pallas_call(
        _proj_kernel,
        grid=(n // tm,),
        in_specs=[row(d), full(g), per_b, per_b, full(wp), full(qn), full(kvn), full(wuq), full(wuqs),
                  full(wukk), full(wukv), row(HB), row(HB)],
        out_specs=[row(w) for w, _ in outs],
        out_shape=[jax.ShapeDtypeStruct((n, w), dt) for w, dt in outs],
        compiler_params=_cparams(("arbitrary",)),
        name="in_proj",
    )(x2, g, sc, sh, wp, qn, kvn, wuq, wuqs, wukk, wukv, ctab, stab)


def _pair_mask(x_pair, odd):
    lane = lax.broadcasted_iota(I32, x_pair.shape, 1)
    keep = (lane >= A_HEAD_DIM) if odd else (lane < A_HEAD_DIM)
    return jnp.where(keep, x_pair, jnp.zeros_like(x_pair))


def _flash_step(hd, s, v_pair, m_ref, l_ref, acc_ref):
    m_prev = m_ref[hd]
    m_new = jnp.maximum(m_prev, jnp.max(s, axis=1, keepdims=True))
    m_safe = jnp.where(m_new == NEG_INF, 0.0, m_new)
    alpha = jnp.exp(m_prev - m_safe)
    p = jnp.exp(s - jnp.concatenate([m_safe] * (s.shape[1] // LANES), axis=1))
    l_ref[hd] = alpha * l_ref[hd] + jnp.sum(p, axis=1, keepdims=True)
    acc_ref[hd] = alpha * acc_ref[hd] + jnp.dot(p.astype(BF16), v_pair, preferred_element_type=F32)
    m_ref[hd] = m_new


def _flash_init(m_ref, l_ref, acc_ref):
    m_ref[...] = jnp.full(m_ref.shape, NEG_INF, F32)
    l_ref[...] = jnp.zeros(l_ref.shape, F32)
    acc_ref[...] = jnp.zeros(acc_ref.shape, F32)


def _flash_finish(y_ref, l_ref, acc_ref, n_heads):
    lane = lax.broadcasted_iota(I32, acc_ref.shape[1:], 1)
    for j in range(n_heads // 2):
        even = acc_ref[2 * j] / l_ref[2 * j]
        odd = acc_ref[2 * j + 1] / l_ref[2 * j + 1]
        y_ref[:, j * LANES:(j + 1) * LANES] = jnp.where(lane < V_DIM, even, odd).astype(y_ref.dtype)


def _t5_bias_heads(rel, bias_ref):
    nb = N_BUCKETS // 2
    n = jnp.abs(rel)
    bucket = jnp.minimum(n, _T5_EXACT)
    for t in _T5_THRESH:
        bucket = bucket + (n >= t).astype(I32)
    bucket = bucket + jnp.where(rel > 0, nb, 0)
    outs = [jnp.full(rel.shape, bias_ref[hd], F32) for hd in range(A_HEADS)]
    for j in range(1, N_BUCKETS):
        hit = bucket == j
        outs = [jnp.where(hit, bias_ref[j * A_HEADS + hd], outs[hd]) for hd in range(A_HEADS)]
    return outs


def _attn_a_kernel(nk_ref, qlo_ref, qhi_ref, klo_ref, khi_ref,
                   bias_ref, qa_ref, qi_ref, wi_ref, ka_ref, va_ref, ki_ref, pq_ref, pk_ref,
                   y_ref,
                   key_ref, am_ref, ex_ref, j_ref, wb_ref, m_ref, l_ref, acc_ref,
                   *, topk, tq, tk, nq, nkt):
    b = pl.program_id(0)
    qi_idx = pl.program_id(1)
    nk = nk_ref[b * nq + qi_idx]
    pq = pq_ref[...]
    qchunk = jnp.right_shift(pq, int(math.log2(CHUNK)))
    lane_tk = lax.broadcasted_iota(I32, (tq, tk), 1)

    wide = lambda v: jnp.concatenate([v] * (tk // LANES), axis=1)
    for hd in range(IDX_HEADS):
        wb_ref[hd] = jnp.broadcast_to(wi_ref[:, hd:hd + 1], (tq, LANES))

    def score_body(kt, carry):
        ks = ki_ref[pl.ds(pl.multiple_of(kt * tk, tk), tk), :]
        sc = jnp.zeros((tq, tk), F32)
        for hd in range(IDX_HEADS):
            pair = qi_ref[:, (hd // 2) * LANES:(hd // 2 + 1) * LANES]
            r = _dot_t(_pair_mask(pair, hd % 2), ks)
            sc = sc + wide(wb_ref[hd]) * jnp.maximum(r, 0.0)
        bits = pltpu.bitcast(sc, I32)
        key = bits ^ (jnp.right_shift(bits, 31) & jnp.int32(0x7FFFFFFF))
        kchunk = jnp.right_shift(pk_ref[kt], int(math.log2(CHUNK)))
        key_ref[kt] = jnp.where(kchunk <= qchunk, key, jnp.int32(INT_MIN))
        return carry
    lax.fori_loop(0, nk, score_body, 0)

    ones = jnp.ones((LANES, LANES), BF16)

    lane_c = lax.broadcasted_iota(I32, (COUNT_ROWS, LANES), 1)

    def count(pred):
        def body(kt, acc):
            parts = []
            for r0 in range(0, tq, COUNT_ROWS):
                rows = slice(r0, r0 + COUNT_ROWS)
                part = acc[rows]
                for c in range(tk // LANES):
                    piece = key_ref[kt, rows, c * LANES:(c + 1) * LANES]
                    part = part + jnp.where(pred(piece, rows, kt * tk + c * LANES), 1, 0)
                parts.append(part)
            return jnp.concatenate(parts, axis=0)
        part = lax.fori_loop(0, nk, body, jnp.zeros((tq, LANES), I32))
        return jnp.dot(part.astype(F32).astype(BF16), ones, preferred_element_type=F32)

    def bit_body(i, t):
        cand = t + jnp.left_shift(jnp.int32(1), 31 - i)
        c = count(lambda k, rows, k0: k >= cand[rows])
        return jnp.where(c >= topk, cand, t)
    thr = lax.fori_loop(0, 32, bit_body, jnp.full((tq, LANES), INT_MIN, I32))
    thr_w = wide(thr)
    need = topk - count(lambda k, rows, k0: k > thr[rows])
    n_eq = count(lambda k, rows, k0: k == thr[rows])
    real = thr > jnp.int32(INT_MIN)
    j_ref[...] = jnp.where(real, jnp.int32(nkt * tk), jnp.int32(-1))
    tie = jnp.max(jnp.where(real & (n_eq > need), 1, 0))

    @pl.when(tie > 0)
    def _():
        def idx_body(i, p):
            cand = p + jnp.left_shift(jnp.int32(1), (nkt * tk).bit_length() - 2 - i)
            c = count(lambda k, rows, k0: (k == thr[rows]) & (lane_c + k0 < cand[rows]))
            return jnp.where(c < need, cand, p)
        p = lax.fori_loop(0, (nkt * tk).bit_length() - 1, idx_body, jnp.zeros((tq, LANES), I32))
        j_ref[...] = jnp.where(real & (n_eq > need), p, j_ref[...])

    jsel_w = wide(j_ref[...])

    def mask_body(kt, carry):
        k = key_ref[kt]
        sel = (k > thr_w) | ((k == thr_w) & (lane_tk + kt * tk <= jsel_w))
        am_ref[kt] = jnp.where(sel, 0.0, NEG_INF)
        return carry
    lax.fori_loop(0, nk, mask_body, 0)

    _flash_init(m_ref, l_ref, acc_ref)
    q_lo = qlo_ref[b * nq + qi_idx]
    q_hi = qhi_ref[b * nq + qi_idx]

    def bias_chunk(kt, c):
        cs = slice(c * LANES, (c + 1) * LANES)
        chunk = b * (nkt * tk // LANES) + kt * (tk // LANES) + c
        far_past = khi_ref[chunk] - q_lo <= -_T5_FAR
        far_future = klo_ref[chunk] - q_hi >= _T5_FAR

        @pl.when(far_past)
        def _():
            for hd in range(A_HEADS):
                ex_ref[hd, :, cs] = am_ref[kt, :, cs] + bias_ref[(N_BUCKETS // 2 - 1) * A_HEADS + hd]

        @pl.when(far_future)
        def _():
            for hd in range(A_HEADS):
                ex_ref[hd, :, cs] = am_ref[kt, :, cs] + bias_ref[(N_BUCKETS - 1) * A_HEADS + hd]

        @pl.when(jnp.logical_not(far_past | far_future))
        def _():
            pkc = pk_ref[kt][:, cs]

            def rows_body(r, carry):
                rows = pl.ds(pl.multiple_of(r * BIAS_ROWS, BIAS_ROWS), BIAS_ROWS)
                am = am_ref[kt, rows, cs]
                for hd, bias in enumerate(_t5_bias_heads(pkc - pq_ref[rows, :], bias_ref)):
                    ex_ref[hd, rows, cs] = am + bias
                return carry
            lax.fori_loop(0, tq // BIAS_ROWS, rows_body, 0)

    def attn_body(kt, carry):
        start = pl.multiple_of(kt * tk, tk)
        for c in range(tk // LANES):
            bias_chunk(kt, c)

        for hd in range(A_HEADS):
            pr = slice((hd // 2) * LANES, (hd // 2 + 1) * LANES)
            qm = _pair_mask(qa_ref[:, pr], hd % 2)
            s = _dot_t(qm, ka_ref[pl.ds(start, tk), pr]) + ex_ref[hd]
            _flash_step(hd, s, va_ref[pl.ds(start, tk), pr], m_ref, l_ref, acc_ref)
        return carry
    lax.fori_loop(0, nk, attn_body, 0)
    _flash_finish(y_ref, l_ref, acc_ref, A_HEADS)


def _tile_tables(positions, tq, tk):
    b, s = positions.shape
    pq = positions.reshape(b, s // tq, tq)
    pk = positions.reshape(b, s // tk, tk)
    pc = positions.reshape(b, s // LANES, LANES)
    q_lo, q_hi = pq.min(-1), pq.max(-1)
    vis = (pk.min(-1) // CHUNK)[:, None, :] <= (q_hi // CHUNK)[:, :, None]
    last = jnp.max(jnp.where(vis, jnp.arange(s // tk, dtype=I32)[None, None, :] + 1, 0), axis=-1)
    flat = lambda a: a.reshape(-1).astype(I32)
    return flat(last), flat(q_lo), flat(q_hi), flat(pc.min(-1)), flat(pc.max(-1))


def _attn_a(qa, qi, wi, ka, va, ki, positions, rel_bias, tq=256, tk=512):
    b, s = positions.shape
    n = b * s
    nq, nkt = s // tq, s // tk
    topk = min(TOPK_MAX, s // 4)
    tables = _tile_tables(positions, tq, tk)
    pos_col = positions.reshape(n, 1)
    pos_row = positions.reshape(b, nkt, 1, tk)
    qrow = lambda w: pl.BlockSpec((tq, w), lambda bi, i, *_: (bi * nq + i, 0))
    kfull = lambda w: pl.BlockSpec((s, w), lambda bi, i, *_: (bi, 0))
    kern = functools.partial(_attn_a_kernel, topk=topk, tq=tq, tk=tk, nq=nq, nkt=nkt)
    grid_spec = pltpu.PrefetchScalarGridSpec(
        num_scalar_prefetch=5,
        grid=(b, nq),
        in_specs=[pl.BlockSpec(memory_space=pltpu.SMEM),
                  qrow(D_A), qrow(D_A), qrow(LANES), kfull(D_A), kfull(D_A), kfull(LANES),
                  pl.BlockSpec((tq, 1), lambda bi, i, *_: (bi * nq + i, 0)),
                  pl.BlockSpec((None, nkt, 1, tk), lambda bi, i, *_: (bi, 0, 0, 0))],
        out_specs=qrow(D_A),
        scratch_shapes=[pltpu.VMEM((nkt, tq, tk), I32), pltpu.VMEM((nkt, tq, tk), F32),
                        pltpu.VMEM((A_HEADS, tq, tk), F32), pltpu.VMEM((tq, LANES), I32),
                        pltpu.VMEM((IDX_HEADS, tq, LANES), F32),
                        pltpu.VMEM((A_HEADS, tq, LANES), F32), pltpu.VMEM((A_HEADS, tq, LANES), F32),
                        pltpu.VMEM((A_HEADS, tq, LANES), F32)])
    return pl.pallas_call(
        kern, grid_spec=grid_spec,
        out_shape=jax.ShapeDtypeStruct((n, D_A), BF16),
        compiler_params=_cparams(("arbitrary", "arbitrary")),
        name="attn_indexer",
    )(*tables, rel_bias.reshape(-1), qa, qi, wi, ka, va, ki, pos_col, pos_row)


def _attn_b_kernel(nk_ref, qb_ref, kb_ref, vb_ref, pq_ref, pk_ref, y_ref, m_ref, l_ref, acc_ref, *, tq, tk, nq):
    b = pl.program_id(0)
    nk = nk_ref[b * nq + pl.program_id(1)]
    qchunk = jnp.right_shift(pq_ref[...], int(math.log2(CHUNK)))
    _flash_init(m_ref, l_ref, acc_ref)

    def body(kt, carry):
        start = pl.multiple_of(kt * tk, tk)
        kchunk = jnp.right_shift(pk_ref[kt], int(math.log2(CHUNK)))
        am = jnp.where(kchunk <= qchunk, 0.0, NEG_INF)
        for hd in range(B_HEADS):
            hs = slice(hd * HB, (hd + 1) * HB)
            pr = slice((hd // 2) * LANES, (hd // 2 + 1) * LANES)
            s = _dot_t(qb_ref[:, hs], kb_ref[pl.ds(start, tk), hs]) + am
            _flash_step(hd, s, vb_ref[pl.ds(start, tk), pr], m_ref, l_ref, acc_ref)
        return carry
    lax.fori_loop(0, nk, body, 0)
    _flash_finish(y_ref, l_ref, acc_ref, B_HEADS)


def _attn_b(qb, kb, vb, positions, tq=256, tk=512):
    b, s = positions.shape
    n = b * s
    nq, nkt = s // tq, s // tk
    nk = _tile_tables(positions, tq, tk)[0]
    qrow = lambda w: pl.BlockSpec((tq, w), lambda bi, i, *_: (bi * nq + i, 0))
    kfull = lambda w: pl.BlockSpec((s, w), lambda bi, i, *_: (bi, 0))
    grid_spec = pltpu.PrefetchScalarGridSpec(
        num_scalar_prefetch=1,
        grid=(b, nq),
        in_specs=[qrow(B_HEADS * HB), kfull(B_HEADS * HB), kfull(D_B),
                  pl.BlockSpec((tq, 1), lambda bi, i, *_: (bi * nq + i, 0)),
                  pl.BlockSpec((None, nkt, 1, tk), lambda bi, i, *_: (bi, 0, 0, 0))],
        out_specs=qrow(D_B),
        scratch_shapes=[pltpu.VMEM((B_HEADS, tq, LANES), F32)] * 3)
    return pl.pallas_call(
        functools.partial(_attn_b_kernel, tq=tq, tk=tk, nq=nq), grid_spec=grid_spec,
        out_shape=jax.ShapeDtypeStruct((n, D_B), BF16),
        compiler_params=_cparams(("arbitrary", "arbitrary")),
        name="attn_latent",
    )(nk, qb, kb, vb, positions.reshape(n, 1), positions.reshape(b, nkt, 1, tk))


def _out_router_kernel(x_ref, ya_ref, yb_ref, wo_ref, gm_ref, g_ref, sc_ref, sh_ref, wr_ref, br_ref,
                       x1_ref, h2_ref, e_ref, gate_ref, rank_ref, cnt_ref, run_ref, *, n_experts):
    @pl.when(pl.program_id(0) == 0)
    def _():
        run_ref[...] = jnp.zeros(run_ref.shape, F32)

    mix = (jnp.dot(ya_ref[...], wo_ref[:D_A, :], preferred_element_type=F32)
           + jnp.dot(yb_ref[...], wo_ref[D_A:, :], preferred_element_type=F32))
    x1 = x_ref[...] + gm_ref[0] * mix
    x1_ref[...] = x1
    h2 = _rms(x1, g_ref[...]) * (1.0 + sc_ref[0]) + sh_ref[0]
    h2_ref[...] = h2
    logits = jnp.dot(h2, wr_ref[...], preferred_element_type=F32, precision=lax.Precision.HIGHEST) + br_ref[...]
    lane = lax.broadcasted_iota(I32, logits.shape, 1)
    cur = jnp.where(lane < n_experts, logits, NEG_INF)
    e_out = jnp.zeros(logits.shape, I32)
    g_out = jnp.zeros(logits.shape, F32)
    top = None
    picks = []
    for k in range(TOP_K):
        m = jnp.max(cur, axis=1, keepdims=True)
        idx = jnp.min(jnp.where(cur == m, lane, LANES), axis=1, keepdims=True)
        top = m if top is None else top
        e_out = jnp.where(lane == k, idx, e_out)
        g_out = jnp.where(lane == k, jnp.exp(m - top), g_out)
        picks.append(lane == idx)
        cur = jnp.where(picks[-1], NEG_INF, cur)
    e_ref[...] = e_out
    gate_ref[...] = g_out / jnp.sum(g_out, axis=1, keepdims=True)
    tm = logits.shape[0]
    chosen = jnp.where(picks[0] | picks[1] | picks[2] | picks[3], 1.0, 0.0)
    earlier = (lax.broadcasted_iota(I32, (tm, tm), 0) > lax.broadcasted_iota(I32, (tm, tm), 1))
    before = jnp.dot(jnp.where(earlier, 1.0, 0.0).astype(BF16), chosen.astype(BF16),
                     preferred_element_type=F32) + run_ref[0:1, :]
    r_out = jnp.zeros(logits.shape, I32)
    for k in range(TOP_K):
        rk = jnp.sum(jnp.where(picks[k], before, 0.0), axis=1, keepdims=True)
        r_out = jnp.where(lane == k, rk.astype(I32), r_out)
    rank_ref[...] = r_out
    run_ref[...] = run_ref[...] + jnp.sum(chosen, axis=0, keepdims=True)
    cnt_ref[...] = run_ref[...].astype(I32)


def _out_router(x2, ya, yb, wo, gm, g, sc, sh, wr, br, seq, tm=256):
    n, d = x2.shape
    tpb = seq // tm
    n_experts = wr.shape[1]
    wr_p = jnp.pad(wr, ((0, 0), (0, LANES - n_experts)))
    br_p = jnp.pad(br, (0, LANES - n_experts)).reshape(1, LANES)
    row = lambda w: pl.BlockSpec((tm, w), lambda i: (i, 0))
    full = lambda a: pl.BlockSpec(a.shape, lambda i: (0,) * a.ndim)
    per_b = pl.BlockSpec((1, 1, d), lambda i: (i // tpb, 0, 0))
    return pl.pallas_call(
        functools.partial(_out_router_kernel, n_experts=n_experts),
        grid=(n // tm,),
        in_specs=[row(d), row(D_A), row(D_B), full(wo), per_b, full(g), per_b, per_b, full(wr_p), full(br_p)],
        out_specs=[row(d), row(d), row(LANES), row(LANES), row(LANES),
                   pl.BlockSpec((SUBLANES, LANES), lambda i: (0, 0))],
        out_shape=[jax.ShapeDtypeStruct((n, d), F32), jax.ShapeDtypeStruct((n, d), F32),
                   jax.ShapeDtypeStruct((n, LANES), I32), jax.ShapeDtypeStruct((n, LANES), F32),
                   jax.ShapeDtypeStruct((n, LANES), I32), jax.ShapeDtypeStruct((SUBLANES, LANES), I32)],
        scratch_shapes=[pltpu.VMEM((SUBLANES, LANES), F32)],
        compiler_params=_cparams(("arbitrary",)),
        name="out_proj_router",
    )(x2, ya, yb, wo, gm, g, sc, sh, wr_p, br_p)


def _row_copy(src_hbm, row, dst_ref, slot, sem):
    return pltpu.make_async_copy(src_hbm.at[pl.ds(row, 1), :], dst_ref.at[pl.ds(slot, 1), :], sem)


def _moe_dispatch_kernel(dest_ref, pend_ref, h_ref, xs_hbm, zero_ref, zsem, sem, *, tm, n_experts):
    i = pl.program_id(0)

    @pl.when(i == 0)
    def _():
        zero_ref[...] = jnp.zeros(zero_ref.shape, zero_ref.dtype)

        def tail(e):
            start = pl.multiple_of(pend_ref[e + 1] - MOE_BLOCK, MOE_BLOCK)
            return pltpu.make_async_copy(zero_ref, xs_hbm.at[pl.ds(start, MOE_BLOCK), :], zsem)

        def has_slots(e):
            return pend_ref[e + 1] > pend_ref[e]

        def unused(j):
            start = pl.multiple_of(pend_ref[n_experts] + j * MOE_BLOCK, MOE_BLOCK)
            return pltpu.make_async_copy(zero_ref, xs_hbm.at[pl.ds(start, MOE_BLOCK), :], zsem)

        def is_unused(j):
            return pend_ref[n_experts] + j * MOE_BLOCK < xs_hbm.shape[0]
        for e in range(n_experts):
            pl.when(has_slots(e))(lambda e=e: tail(e).start())
        for j in range(n_experts):
            pl.when(is_unused(j))(lambda j=j: unused(j).start())
        for e in range(n_experts):
            pl.when(has_slots(e))(lambda e=e: tail(e).wait())
        for j in range(n_experts):
            pl.when(is_unused(j))(lambda j=j: unused(j).wait())

    base = i * tm * TOP_K

    def issue(t, carry):
        for k in range(TOP_K):
            pltpu.make_async_copy(h_ref.at[pl.ds(t, 1), :],
                                  xs_hbm.at[pl.ds(dest_ref[base + t * TOP_K + k], 1), :], sem).start(priority=k % 2)
        return carry
    lax.fori_loop(0, tm, issue, 0, unroll=2)
    for k in range(TOP_K):
        pltpu.make_async_copy(h_ref, xs_hbm.at[pl.ds(0, tm), :], sem).wait()


def _moe_dispatch(dest, pend0, h2, cap, tm=512):
    n, d = h2.shape
    n_experts = pend0.shape[0] - 1
    grid_spec = pltpu.PrefetchScalarGridSpec(
        num_scalar_prefetch=2, grid=(n // tm,),
        in_specs=[pl.BlockSpec((tm, d), lambda i, *_: (i, 0))],
        out_specs=pl.BlockSpec(memory_space=pl.ANY),
        scratch_shapes=[pltpu.VMEM((MOE_BLOCK, d), h2.dtype), pltpu.SemaphoreType.DMA, pltpu.SemaphoreType.DMA])
    return pl.pallas_call(
        functools.partial(_moe_dispatch_kernel, tm=tm, n_experts=n_experts), grid_spec=grid_spec,
        out_shape=jax.ShapeDtypeStruct((cap, d), h2.dtype),
        compiler_params=_cparams(("arbitrary",)),
        name="moe_dispatch",
    )(dest, pend0, h2)


def _moe_expert_kernel(be_ref, nu_ref, xs_ref, w1_ref, b1_ref, w2_ref, b2_ref, y_ref, w1b_ref, w2b_ref, *, d_exp):
    i = pl.program_id(0)
    live = i < nu_ref[0]
    changed = jnp.logical_or(i == 0, be_ref[i] != be_ref[jnp.maximum(i - 1, 0)])

    @pl.when(live & changed)
    def _():
        w1b_ref[...] = w1_ref[0].astype(BF16)
        w2b_ref[...] = w2_ref[0].astype(BF16)

    @pl.when(live)
    def _():
        hid = jnp.dot(xs_ref[...].astype(BF16), w1b_ref[...], preferred_element_type=F32) + b1_ref[0]
        glu = jnp.minimum(hid[:, :d_exp], SWIGLU_LIMIT)
        lin = jnp.clip(hid[:, d_exp:], -SWIGLU_LIMIT, SWIGLU_LIMIT)
        act = glu * jax.nn.sigmoid(SWIGLU_ALPHA * glu) * (lin + 1.0)
        y_ref[...] = jnp.dot(act.astype(BF16), w2b_ref[...], preferred_element_type=F32) + b2_ref[0]

    @pl.when(jnp.logical_not(live))
    def _():
        y_ref[...] = jnp.zeros(y_ref.shape, y_ref.dtype)


def _moe_experts(block_e, n_used, xs, w1, b1, w2, b2):
    cap, d = xs.shape
    d2 = w1.shape[-1]
    d_exp = d2 // 2
    n_exp = w1.shape[0] * w1.shape[1]
    w1 = w1.reshape(n_exp, d, d2)
    w2 = w2.reshape(n_exp, d_exp, d)
    grid_spec = pltpu.PrefetchScalarGridSpec(
        num_scalar_prefetch=2, grid=(cap // MOE_BLOCK,),
        in_specs=[pl.BlockSpec((MOE_BLOCK, d), lambda i, be, nu: (jnp.minimum(i, nu[0] - 1), 0)),
                  pl.BlockSpec((1, d, d2), lambda i, be, nu: (be[i], 0, 0)),
                  pl.BlockSpec((1, 1, d2), lambda i, be, nu: (be[i], 0, 0)),
                  pl.BlockSpec((1, d_exp, d), lambda i, be, nu: (be[i], 0, 0)),
                  pl.BlockSpec((1, 1, d), lambda i, be, nu: (be[i], 0, 0))],
        out_specs=pl.BlockSpec((MOE_BLOCK, d), lambda i, be, nu: (i, 0)),
        scratch_shapes=[pltpu.VMEM((d, d2), BF16), pltpu.VMEM((d_exp, d), BF16)])
    return pl.pallas_call(
        functools.partial(_moe_expert_kernel, d_exp=d_exp), grid_spec=grid_spec,
        out_shape=jax.ShapeDtypeStruct((cap, d), F32),
        compiler_params=_cparams(("arbitrary",)),
        name="moe_experts",
    )(block_e, n_used, xs, w1, b1.reshape(n_exp, 1, d2), w2, b2.reshape(n_exp, 1, d))


def _moe_combine_kernel(dest_ref, ys_hbm, x1_ref, gate_ref, gf_ref, gfin_ref, o_ref, rows_ref, sem, *, tm, final):
    base = pl.program_id(0) * tm * TOP_K

    def issue(t, carry):
        for k in range(TOP_K):
            _row_copy(ys_hbm, dest_ref[base + t * TOP_K + k], rows_ref.at[k], t, sem).start()
        return carry
    lax.fori_loop(0, tm, issue, 0, unroll=2)
    for k in range(TOP_K):
        pltpu.make_async_copy(ys_hbm.at[pl.ds(0, tm), :], rows_ref.at[k], sem).wait()
    moe = gate_ref[:, 0:1] * rows_ref[0]
    for k in range(1, TOP_K):
        moe = moe + gate_ref[:, k:k + 1] * rows_ref[k]
    out = x1_ref[...] + gf_ref[0] * moe
    if final:
        out = _rms(out, gfin_ref[...])
    o_ref[...] = out


def _moe_combine(dest, ys, x1, gate, gf, g_final, seq, final, tm=256):
    n, d = x1.shape
    tpb = seq // tm
    grid_spec = pltpu.PrefetchScalarGridSpec(
        num_scalar_prefetch=1, grid=(n // tm,),
        in_specs=[pl.BlockSpec(memory_space=pl.ANY),
                  pl.BlockSpec((tm, d), lambda i, *_: (i, 0)),
                  pl.BlockSpec((tm, LANES), lambda i, *_: (i, 0)),
                  pl.BlockSpec((1, 1, d), lambda i, *_: (i // tpb, 0, 0)),
                  pl.BlockSpec((1, d), lambda i, *_: (0, 0))],
        out_specs=pl.BlockSpec((tm, d), lambda i, *_: (i, 0)),
        scratch_shapes=[pltpu.VMEM((TOP_K, tm, d), F32), pltpu.SemaphoreType.DMA])
    return pl.pallas_call(
        functools.partial(_moe_combine_kernel, tm=tm, final=final), grid_spec=grid_spec,
        out_shape=jax.ShapeDtypeStruct((n, d), F32),
        compiler_params=_cparams(("arbitrary",)),
        name="moe_combine",
    )(dest, ys, x1, gate, gf, g_final)


def _route_tables(top_e, rank, counts, n_blocks):
    n_experts = counts.shape[0]
    padded = (counts + MOE_BLOCK - 1) // MOE_BLOCK * MOE_BLOCK
    pend = jnp.cumsum(padded)
    pstart = pend - padded
    onehot = top_e[..., None] == jnp.arange(n_experts, dtype=I32)
    dest = (jnp.sum(jnp.where(onehot, pstart, 0), axis=-1) + rank).reshape(-1).astype(I32)
    block_e = jnp.minimum(jnp.sum(pend[None, :] <= jnp.arange(n_blocks, dtype=I32)[:, None] * MOE_BLOCK, axis=1),
                          n_experts - 1).astype(I32)
    n_used = (pend[-1:] // MOE_BLOCK).astype(I32)
    pend0 = jnp.concatenate([jnp.zeros((1,), I32), pend.astype(I32)])
    return dest, pend0, block_e, n_used


def kernel(x, c, positions, rel_bias, norm_mix, w_ada, b_ada, w_in, q_norm, w_uq, kv_norm, w_ukv, w_out, norm_ffn, w_router, b_router, w1, b1, w2, b2, norm_final):
    bsz, seq, d = x.shape
    depth = w_ada.shape[0]
    n = bsz * seq
    n_experts = w_router.shape[-1]
    mod = _ada_mod(c, w_ada, b_ada)
    ctab, stab = _rope_tables(positions)
    row = lambda v: v.reshape(1, -1)
    x2 = x.reshape(n, d)
    for l in range(depth):
        sh_m, sc_m, g_m, sh_f, sc_f, g_f = [mod[l, :, i * d:(i + 1) * d].reshape(bsz, 1, d) for i in range(6)]
        wuq, wuqs = _pack_uq(w_uq[l])
        wukk, wukv = _pack_ukv(w_ukv[l])
        qa, ka, va, qi, ki, wi, qb, kb, vb = _proj(
            x2, row(norm_mix[l]), sc_m, sh_m, _pack_in_weights(w_in[l]), row(q_norm[l]), row(kv_norm[l]),
            wuq, wuqs, wukk, wukv, ctab, stab, seq)
        ya = _attn_a(qa, qi, wi, ka, va, ki, positions, rel_bias)
        yb = _attn_b(qb, kb, vb, positions)
        x1, h2, top_e, gate, rank, counts = _out_router(x2, ya, yb, w_out[l].astype(BF16), g_m, row(norm_ffn[l]),
                                                        sc_f, sh_f, w_router[l], b_router[l], seq)
        n_blocks = -(-n * TOP_K // MOE_BLOCK) + n_experts
        dest, pend0, block_e, n_used = _route_tables(top_e[:, :TOP_K], rank[:, :TOP_K], counts[0, :n_experts],
                                                     n_blocks)
        xs = _moe_dispatch(dest, pend0, h2, n_blocks * MOE_BLOCK)
        ys = _moe_experts(block_e + l * n_experts, n_used, xs, w1, b1, w2, b2)
        x2 = _moe_combine(dest, ys, x1, gate, g_f, row(norm_final), seq, final=(l == depth - 1))
    return x2.reshape(bsz, seq, d)
```

```python
import functools
import math

import numpy as np
import jax
import jax.numpy as jnp
from jax import lax
from jax.experimental import pallas as pl
from jax.experimental.pallas import tpu as pltpu

CHUNK = 64
EPS = 1e-6
A_HEADS = 8
A_HEAD_DIM = 64
IDX_HEADS = 8
IDX_DIM = 64
TOPK_MAX = 256
B_HEADS = 8
Q_LORA = 256
KV_LORA = 128
QK_NOPE = 64
QK_ROPE = 32
V_DIM = 64
ROPE_THETA = 10000.0
N_BUCKETS = 32
MAX_DISTANCE = 128
TOP_K = 4
SWIGLU_LIMIT = 7.0
SWIGLU_ALPHA = 1.702
MOE_BLOCK = 256

LANES = 128
SUBLANES = 8
VMEM_LIMIT = 56 * 1024 * 1024
INT_MIN = -2 ** 31
NEG_INF = float("-inf")
BIAS_ROWS = 32

F32 = jnp.float32
BF16 = jnp.bfloat16
I32 = jnp.int32

_T5_EXACT = (N_BUCKETS // 2) // 2
_T5_THRESH = tuple(
    int(math.ceil(_T5_EXACT * (MAX_DISTANCE / _T5_EXACT) ** (k / (N_BUCKETS // 2 - _T5_EXACT)) - 1e-9))
    for k in range(1, N_BUCKETS // 2 - _T5_EXACT))
_T5_FAR = _T5_THRESH[-1]


def _cparams(sem, vmem=VMEM_LIMIT, **kw):
    return pltpu.CompilerParams(dimension_semantics=sem, vmem_limit_bytes=vmem, **kw)


def _dot_t(a, b):
    return lax.dot_general(a, b, (((1,), (1,)), ((), ())), preferred_element_type=F32)


def _ada_kernel(c_ref, w_ref, b_ref, o_ref):
    c = c_ref[...]
    cond = c * jax.nn.sigmoid(c)
    o_ref[0] = jnp.dot(cond, w_ref[0], preferred_element_type=F32,
                       precision=lax.Precision.HIGHEST) + b_ref[0]


def _ada_mod(c, w_ada, b_ada, tn=512):
    depth, d, n6 = w_ada.shape
    b = c.shape[0]
    return pl.pallas_call(
        _ada_kernel,
        grid=(depth, n6 // tn),
        in_specs=[pl.BlockSpec((b, d), lambda l, j: (0, 0)),
                  pl.BlockSpec((1, d, tn), lambda l, j: (l, 0, j)),
                  pl.BlockSpec((1, 1, tn), lambda l, j: (l, 0, j))],
        out_specs=pl.BlockSpec((1, b, tn), lambda l, j: (l, 0, j)),
        out_shape=jax.ShapeDtypeStruct((depth, b, n6), F32),
        compiler_params=_cparams(("arbitrary", "arbitrary")),
        name="ada_mod",
    )(c, w_ada, b_ada.reshape(depth, 1, n6))


D_A = A_HEADS * A_HEAD_DIM
D_B = B_HEADS * V_DIM
HB = 128
_C_QA, _C_KA, _C_VA, _C_QI = 0, D_A, 2 * D_A, 3 * D_A
_C_KI = 4 * D_A
_C_WI = _C_KI + LANES
_C_CQ = _C_WI + LANES
_C_CKV = _C_CQ + Q_LORA
_C_KR = _C_CKV + KV_LORA
_C_KRS = _C_KR + LANES
_C_END = _C_KRS + LANES


def _pack_in_weights(w_in):
    d = w_in.shape[0]
    offs = np.cumsum((0, D_A, D_A, D_A, IDX_HEADS * IDX_DIM, IDX_DIM, IDX_HEADS, Q_LORA, KV_LORA, QK_ROPE))
    seg = [w_in[:, offs[i]:offs[i + 1]] for i in range(9)]
    q_a, k_a, v_a, q_i, k_i, w_i, c_q, c_kv, k_r = seg
    z = lambda n: jnp.zeros((d, n), w_in.dtype)
    half = QK_ROPE // 2
    k_rs = jnp.concatenate([k_r[:, half:], k_r[:, :half]], axis=1)
    cols = [q_a * (A_HEAD_DIM ** -0.5), k_a, v_a, q_i, k_i, k_i, w_i, z(LANES - IDX_HEADS), c_q, c_kv,
            z(QK_NOPE), k_r, z(HB - QK_NOPE - QK_ROPE), z(QK_NOPE), k_rs, z(HB - QK_NOPE - QK_ROPE)]
    return jnp.concatenate(cols, axis=1).astype(BF16)


def _pack_uq(w_uq):
    r = w_uq.shape[0]
    w = w_uq.reshape(r, B_HEADS, QK_NOPE + QK_ROPE)
    nope, rope = w[..., :QK_NOPE], w[..., QK_NOPE:]
    half = QK_ROPE // 2
    z = jnp.zeros((r, B_HEADS, HB - QK_NOPE - QK_ROPE), w.dtype)
    main = jnp.concatenate([nope, rope, z], axis=-1).reshape(r, B_HEADS * HB)
    swap = jnp.concatenate([jnp.zeros_like(nope), rope[..., half:], rope[..., :half], z], axis=-1)
    return main.astype(BF16), swap.reshape(r, B_HEADS * HB).astype(BF16)


def _pack_ukv(w_ukv):
    r = w_ukv.shape[0]
    w = w_ukv.reshape(r, B_HEADS, QK_NOPE + V_DIM)
    k = jnp.concatenate([w[..., :QK_NOPE], jnp.zeros((r, B_HEADS, HB - QK_NOPE), w.dtype)], axis=-1)
    v = w[..., QK_NOPE:]
    return k.reshape(r, B_HEADS * HB).astype(BF16), v.reshape(r, B_HEADS * V_DIM).astype(BF16)


def _rope_tables(positions):
    half = QK_ROPE // 2
    inv = ROPE_THETA ** (-jnp.arange(half, dtype=F32) / half)
    ang = positions.astype(F32).reshape(-1, 1) * inv[None, :]
    cos, sin = jnp.cos(ang), jnp.sin(ang)
    n = ang.shape[0]
    one = jnp.ones((n, QK_NOPE), F32)
    z = lambda k: jnp.zeros((n, k), F32)
    ctab = jnp.concatenate([one, cos, cos, z(HB - QK_NOPE - QK_ROPE)], axis=1)
    stab = jnp.concatenate([z(QK_NOPE), -sin, sin, z(HB - QK_NOPE - QK_ROPE)], axis=1)
    return ctab, stab


def _rms(x, g):
    return x * lax.rsqrt(jnp.mean(x * x, axis=-1, keepdims=True) + EPS) * g


def _proj_kernel(x_ref, g_ref, sc_ref, sh_ref, wp_ref, qn_ref, kvn_ref, wuq_ref, wuqs_ref, wukk_ref, wukv_ref,
                 ct_ref, st_ref,
                 qa_ref, ka_ref, va_ref, qi_ref, ki_ref, wi_ref, qb_ref, kb_ref, vb_ref):
    x = x_ref[...]
    h = _rms(x, g_ref[...]) * (1.0 + sc_ref[0]) + sh_ref[0]
    hb = h.astype(BF16)
    seg = lambda a, b: jnp.dot(hb, wp_ref[:, a:b], preferred_element_type=F32)
    qa_ref[...] = seg(_C_QA, _C_KA).astype(BF16)
    ka_ref[...] = seg(_C_KA, _C_VA).astype(BF16)
    va_ref[...] = seg(_C_VA, _C_QI).astype(BF16)
    qi_ref[...] = seg(_C_QI, _C_KI).astype(BF16)
    ki_ref[...] = seg(_C_KI, _C_WI).astype(BF16)
    wi_ref[...] = seg(_C_WI, _C_CQ)
    ct = ct_ref[...]
    st = st_ref[...]
    cq = _rms(seg(_C_CQ, _C_CKV), qn_ref[...]).astype(BF16)
    scale = (QK_NOPE + QK_ROPE) ** -0.5
    for hd in range(B_HEADS):
        sl = slice(hd * HB, (hd + 1) * HB)
        q = jnp.dot(cq, wuq_ref[:, sl], preferred_element_type=F32)
        qs = jnp.dot(cq, wuqs_ref[:, sl], preferred_element_type=F32)
        qb_ref[:, sl] = ((q * ct + qs * st) * scale).astype(BF16)
    ckv = _rms(seg(_C_CKV, _C_KR), kvn_ref[...]).astype(BF16)
    kr = seg(_C_KR, _C_KRS) * ct + seg(_C_KRS, _C_END) * st
    for hd in range(B_HEADS):
        sl = slice(hd * HB, (hd + 1) * HB)
        kb_ref[:, sl] = (jnp.dot(ckv, wukk_ref[:, sl], preferred_element_type=F32) + kr).astype(BF16)
    vb_ref[...] = jnp.dot(ckv, wukv_ref[...], preferred_element_type=F32).astype(BF16)


def _proj(x2, g, sc, sh, wp, qn, kvn, wuq, wuqs, wukk, wukv, ctab, stab, seq, tm=256):
    n, d = x2.shape
    tpb = seq // tm
    row = lambda w: pl.BlockSpec((tm, w), lambda i: (i, 0))
    full = lambda a: pl.BlockSpec(a.shape, lambda i: (0,) * a.ndim)
    per_b = pl.BlockSpec((1, 1, d), lambda i: (i // tpb, 0, 0))
    outs = [(D_A, BF16), (D_A, BF16), (D_A, BF16), (D_A, BF16), (LANES, BF16), (LANES, F32),
            (B_HEADS * HB, BF16), (B_HEADS * HB, BF16), (D_B, BF16)]
    return pl.pallas_call(
        _proj_kernel,
        grid=(n // tm,),
        in_specs=[row(d), full(g), per_b, per_b, full(wp), full(qn), full(kvn), full(wuq), full(wuqs),
                  full(wukk), full(wukv), row(HB), row(HB)],
        out_specs=[row(w) for w, _ in outs],
        out_shape=[jax.ShapeDtypeStruct((n, w), dt) for w, dt in outs],
        compiler_params=_cparams(("arbitrary",)),
        name="in_proj",
    )(x2, g, sc, sh, wp, qn, kvn, wuq, wuqs, wukk, wukv, ctab, stab)


def _pair_mask(x_pair, odd):
    lane = lax.broadcasted_iota(I32, x_pair.shape, 1)
    keep = (lane >= A_HEAD_DIM) if odd else (lane < A_HEAD_DIM)
    return jnp.where(keep, x_pair, jnp.zeros_like(x_pair))


def _flash_step(hd, s, v_pair, m_ref, l_ref, acc_ref):
    m_prev = m_ref[hd]
    m_new = jnp.maximum(m_prev, jnp.max(s, axis=1, keepdims=True))
    m_safe = jnp.where(m_new == NEG_INF, 0.0, m_new)
    alpha = jnp.exp(m_prev - m_safe)
    p = jnp.exp(s - jnp.concatenate([m_safe] * (s.shape[1] // LANES), axis=1))
    l_ref[hd] = alpha * l_ref[hd] + jnp.sum(p, axis=1, keepdims=True)
    acc_ref[hd] = alpha * acc_ref[hd] + jnp.dot(p.astype(BF16), v_pair, preferred_element_type=F32)
    m_ref[hd] = m_new


def _flash_init(m_ref, l_ref, acc_ref):
    m_ref[...] = jnp.full(m_ref.shape, NEG_INF, F32)
    l_ref[...] = jnp.zeros(l_ref.shape, F32)
    acc_ref[...] = jnp.zeros(acc_ref.shape, F32)


def _flash_finish(y_ref, l_ref, acc_ref, n_heads):
    lane = lax.broadcasted_iota(I32, acc_ref.shape[1:], 1)
    for j in range(n_heads // 2):
        even = acc_ref[2 * j] / l_ref[2 * j]
        odd = acc_ref[2 * j + 1] / l_ref[2 * j + 1]
        y_ref[:, j * LANES:(j + 1) * LANES] = jnp.where(lane < V_DIM, even, odd).astype(y_ref.dtype)


def _t5_bias_heads(rel, bias_ref):
    nb = N_BUCKETS // 2
    n = jnp.abs(rel)
    bucket = jnp.minimum(n, _T5_EXACT)
    for t in _T5_THRESH:
        bucket = bucket + (n >= t).astype(I32)
    bucket = bucket + jnp.where(rel > 0, nb, 0)
    outs = [jnp.full(rel.shape, bias_ref[hd], F32) for hd in range(A_HEADS)]
    for j in range(1, N_BUCKETS):
        hit = bucket == j
        outs = [jnp.where(hit, bias_ref[j * A_HEADS + hd], outs[hd]) for hd in range(A_HEADS)]
    return outs


def _attn_a_kernel(nk_ref, qlo_ref, qhi_ref, klo_ref, khi_ref,
                   bias_ref, qa_ref, qi_ref, wi_ref, ka_ref, va_ref, ki_ref, pq_ref, pk_ref,
                   y_ref,
                   key_ref, keyt_ref, am_ref, ex_ref, j_ref, wb_ref, m_ref, l_ref, acc_ref,
                   *, topk, tq, tk, nq, nkt):
    b = pl.program_id(0)
    qi_idx = pl.program_id(1)
    nk = nk_ref[b * nq + qi_idx]
    pq = pq_ref[...]
    qchunk = jnp.right_shift(pq, int(math.log2(CHUNK)))
    lane_tk = lax.broadcasted_iota(I32, (tq, tk), 1)

    wide = lambda v: jnp.concatenate([v] * (tk // LANES), axis=1)
    for hd in range(IDX_HEADS):
        wb_ref[hd] = jnp.broadcast_to(wi_ref[:, hd:hd + 1], (tq, LANES))

    def score_body(kt, carry):
        ks = ki_ref[pl.ds(pl.multiple_of(kt * tk, tk), tk), :]
        sc = jnp.zeros((tq, tk), F32)
        for hd in range(IDX_HEADS):
            pair = qi_ref[:, (hd // 2) * LANES:(hd // 2 + 1) * LANES]
            r = _dot_t(_pair_mask(pair, hd % 2), ks)
            sc = sc + wide(wb_ref[hd]) * jnp.maximum(r, 0.0)
        bits = pltpu.bitcast(sc, I32)
        key = bits ^ (jnp.right_shift(bits, 31) & jnp.int32(0x7FFFFFFF))
        kchunk = jnp.right_shift(pk_ref[kt], int(math.log2(CHUNK)))
        key = jnp.where(kchunk <= qchunk, key, jnp.int32(INT_MIN))
        key_ref[kt] = key
        keyt_ref[kt] = key.T
        return carry
    lax.fori_loop(0, nk, score_body, 0)

    groups = tk // SUBLANES
    sub_idx = (lax.broadcasted_iota(I32, (groups, SUBLANES, tq), 0) * SUBLANES
               + lax.broadcasted_iota(I32, (groups, SUBLANES, tq), 1))

    def count(pred):
        def body(kt, acc):
            k3 = keyt_ref[kt].reshape(groups, SUBLANES, tq)
            return acc + jnp.sum(jnp.where(pred(k3, kt * tk), 1, 0), axis=0)
        acc = lax.fori_loop(0, nk, body, jnp.zeros((SUBLANES, tq), I32))
        return jnp.broadcast_to(jnp.sum(acc, axis=0, keepdims=True), (SUBLANES, tq))

    def bit_body(i, t):
        cand = t + jnp.left_shift(jnp.int32(1), 31 - i)
        c = count(lambda k, k0: k >= cand)
        return jnp.where(c >= topk, cand, t)
    thr_t = lax.fori_loop(0, 32, bit_body, jnp.full((SUBLANES, tq), INT_MIN, I32))
    need = topk - count(lambda k, k0: k > thr_t)
    n_eq = count(lambda k, k0: k == thr_t)
    real = thr_t > jnp.int32(INT_MIN)
    j_ref[...] = jnp.where(real, jnp.int32(nkt * tk), jnp.int32(-1))
    tie = jnp.max(jnp.where(real & (n_eq > need), 1, 0))

    @pl.when(tie > 0)
    def _():
        def idx_body(i, p):
            cand = p + jnp.left_shift(jnp.int32(1), (nkt * tk).bit_length() - 2 - i)
            c = count(lambda k, k0: (k == thr_t) & (sub_idx + k0 < cand))
            return jnp.where(c < need, cand, p)
        p = lax.fori_loop(0, (nkt * tk).bit_length() - 1, idx_body, jnp.zeros((SUBLANES, tq), I32))
        j_ref[...] = jnp.where(real & (n_eq > need), p, j_ref[...])

    to_rows = lambda v: jnp.broadcast_to(v[:1], (LANES, tq)).T
    thr_w = wide(to_rows(thr_t))
    jsel_w = wide(to_rows(j_ref[...]))

    def mask_body(kt, carry):
        k = key_ref[kt]
        sel = (k > thr_w) | ((k == thr_w) & (lane_tk + kt * tk <= jsel_w))
        am_ref[kt] = jnp.where(sel, 0.0, NEG_INF)
        return carry
    lax.fori_loop(0, nk, mask_body, 0)

    _flash_init(m_ref, l_ref, acc_ref)
    q_lo = qlo_ref[b * nq + qi_idx]
    q_hi = qhi_ref[b * nq + qi_idx]

    def bias_chunk(kt, c):
        cs = slice(c * LANES, (c + 1) * LANES)
        chunk = b * (nkt * tk // LANES) + kt * (tk // LANES) + c
        far_past = khi_ref[chunk] - q_lo <= -_T5_FAR
        far_future = klo_ref[chunk] - q_hi >= _T5_FAR

        @pl.when(far_past)
        def _():
            for hd in range(A_HEADS):
                ex_ref[hd, :, cs] = am_ref[kt, :, cs] + bias_ref[(N_BUCKETS // 2 - 1) * A_HEADS + hd]

        @pl.when(far_future)
        def _():
            for hd in range(A_HEADS):
                ex_ref[hd, :, cs] = am_ref[kt, :, cs] + bias_ref[(N_BUCKETS - 1) * A_HEADS + hd]

        @pl.when(jnp.logical_not(far_past | far_future))
        def _():
            pkc = pk_ref[kt][:, cs]

            def rows_body(r, carry):
                rows = pl.ds(pl.multiple_of(r * BIAS_ROWS, BIAS_ROWS), BIAS_ROWS)
                am = am_ref[kt, rows, cs]
                for hd, bias in enumerate(_t5_bias_heads(pkc - pq_ref[rows, :], bias_ref)):
                    ex_ref[hd, rows, cs] = am + bias
                return carry
            lax.fori_loop(0, tq // BIAS_ROWS, rows_body, 0)

    def attn_body(kt, carry):
        start = pl.multiple_of(kt * tk, tk)
        for c in range(tk // LANES):
            bias_chunk(kt, c)

        for hd in range(A_HEADS):
            pr = slice((hd // 2) * LANES, (hd // 2 + 1) * LANES)
            qm = _pair_mask(qa_ref[:, pr], hd % 2)
            s = _dot_t(qm, ka_ref[pl.ds(start, tk), pr]) + ex_ref[hd]
            _flash_step(hd, s, va_ref[pl.ds(start, tk), pr], m_ref, l_ref, acc_ref)
        return carry
    lax.fori_loop(0, nk, attn_body, 0)
    _flash_finish(y_ref, l_ref, acc_ref, A_HEADS)


def _tile_tables(positions, tq, tk):
    b, s = positions.shape
    pq = positions.reshape(b, s // tq, tq)
    pk = positions.reshape(b, s // tk, tk)
    pc = positions.reshape(b, s // LANES, LANES)
    q_lo, q_hi = pq.min(-1), pq.max(-1)
    vis = (pk.min(-1) // CHUNK)[:, None, :] <= (q_hi // CHUNK)[:, :, None]
    last = jnp.max(jnp.where(vis, jnp.arange(s // tk, dtype=I32)[None, None, :] + 1, 0), axis=-1)
    flat = lambda a: a.reshape(-1).astype(I32)
    return flat(last), flat(q_lo), flat(q_hi), flat(pc.min(-1)), flat(pc.max(-1))


def _attn_a(qa, qi, wi, ka, va, ki, positions, rel_bias, tq=256, tk=512):
    b, s = positions.shape
    n = b * s
    nq, nkt = s // tq, s // tk
    topk = min(TOPK_MAX, s // 4)
    tables = _tile_tables(positions, tq, tk)
    pos_col = positions.reshape(n, 1)
    pos_row = positions.reshape(b, nkt, 1, tk)
    qrow = lambda w: pl.BlockSpec((tq, w), lambda bi, i, *_: (bi * nq + i, 0))
    kfull = lambda w: pl.BlockSpec((s, w), lambda bi, i, *_: (bi, 0))
    kern = functools.partial(_attn_a_kernel, topk=topk, tq=tq, tk=tk, nq=nq, nkt=nkt)
    grid_spec = pltpu.PrefetchScalarGridSpec(
        num_scalar_prefetch=5,
        grid=(b, nq),
        in_specs=[pl.BlockSpec(memory_space=pltpu.SMEM),
                  qrow(D_A), qrow(D_A), qrow(LANES), kfull(D_A), kfull(D_A), kfull(LANES),
                  pl.BlockSpec((tq, 1), lambda bi, i, *_: (bi * nq + i, 0)),
                  pl.BlockSpec((None, nkt, 1, tk), lambda bi, i, *_: (bi, 0, 0, 0))],
        out_specs=qrow(D_A),
        scratch_shapes=[pltpu.VMEM((nkt, tq, tk), I32), pltpu.VMEM((nkt, tk, tq), I32),
                        pltpu.VMEM((nkt, tq, tk), F32),
                        pltpu.VMEM((A_HEADS, tq, tk), F32), pltpu.VMEM((SUBLANES, tq), I32),
                        pltpu.VMEM((IDX_HEADS, tq, LANES), F32),
                        pltpu.VMEM((A_HEADS, tq, LANES), F32), pltpu.VMEM((A_HEADS, tq, LANES), F32),
                        pltpu.VMEM((A_HEADS, tq, LANES), F32)])
    return pl.pallas_call(
        kern, grid_spec=grid_spec,
        out_shape=jax.ShapeDtypeStruct((n, D_A), BF16),
        compiler_params=_cparams(("arbitrary", "arbitrary")),
        name="attn_indexer",
    )(*tables, rel_bias.reshape(-1), qa, qi, wi, ka, va, ki, pos_col, pos_row)


def _attn_b_kernel(nk_ref, qb_ref, kb_ref, vb_ref, pq_ref, pk_ref, y_ref, m_ref, l_ref, acc_ref, *, tq, tk, nq):
    b = pl.program_id(0)
    nk = nk_ref[b * nq + pl.program_id(1)]
    qchunk = jnp.right_shift(pq_ref[...], int(math.log2(CHUNK)))
    _flash_init(m_ref, l_ref, acc_ref)

    def body(kt, carry):
        start = pl.multiple_of(kt * tk, tk)
        kchunk = jnp.right_shift(pk_ref[kt], int(math.log2(CHUNK)))
        am = jnp.where(kchunk <= qchunk, 0.0, NEG_INF)
        for hd in range(B_HEADS):
            hs = slice(hd * HB, (hd + 1) * HB)
            pr = slice((hd // 2) * LANES, (hd // 2 + 1) * LANES)
            s = _dot_t(qb_ref[:, hs], kb_ref[pl.ds(start, tk), hs]) + am
            _flash_step(hd, s, vb_ref[pl.ds(start, tk), pr], m_ref, l_ref, acc_ref)
        return carry
    lax.fori_loop(0, nk, body, 0)
    _flash_finish(y_ref, l_ref, acc_ref, B_HEADS)


def _attn_b(qb, kb, vb, positions, tq=256, tk=512):
    b, s = positions.shape
    n = b * s
    nq, nkt = s // tq, s // tk
    nk = _tile_tables(positions, tq, tk)[0]
    qrow = lambda w: pl.BlockSpec((tq, w), lambda bi, i, *_: (bi * nq + i, 0))
    kfull = lambda w: pl.BlockSpec((s, w), lambda bi, i, *_: (bi, 0))
    grid_spec = pltpu.PrefetchScalarGridSpec(
        num_scalar_prefetch=1,
        grid=(b, nq),
        in_specs=[qrow(B_HEADS * HB), kfull(B_HEADS * HB), kfull(D_B),
                  pl.BlockSpec((tq, 1), lambda bi, i, *_: (bi * nq + i, 0)),
                  pl.BlockSpec((None, nkt, 1, tk), lambda bi, i, *_: (bi, 0, 0, 0))],
        out_specs=qrow(D_B),
        scratch_shapes=[pltpu.VMEM((B_HEADS, tq, LANES), F32)] * 3)
    return pl.pallas_call(
        functools.partial(_attn_b_kernel, tq=tq, tk=tk, nq=nq), grid_spec=grid_spec,
        out_shape=jax.ShapeDtypeStruct((n, D_B), BF16),
        compiler_params=_cparams(("arbitrary", "arbitrary")),
        name="attn_latent",
    )(nk, qb, kb, vb, positions.reshape(n, 1), positions.reshape(b, nkt, 1, tk))


def _out_router_kernel(x_ref, ya_ref, yb_ref, wo_ref, gm_ref, g_ref, sc_ref, sh_ref, wr_ref, br_ref,
                       x1_ref, h2_ref, e_ref, gate_ref, rank_ref, cnt_ref, run_ref, *, n_experts):
    @pl.when(pl.program_id(0) == 0)
    def _():
        run_ref[...] = jnp.zeros(run_ref.shape, F32)

    mix = (jnp.dot(ya_ref[...], wo_ref[:D_A, :], preferred_element_type=F32)
           + jnp.dot(yb_ref[...], wo_ref[D_A:, :], preferred_element_type=F32))
    x1 = x_ref[...] + gm_ref[0] * mix
    x1_ref[...] = x1
    h2 = _rms(x1, g_ref[...]) * (1.0 + sc_ref[0]) + sh_ref[0]
    h2_ref[...] = h2
    logits = jnp.dot(h2, wr_ref[...], preferred_element_type=F32, precision=lax.Precision.HIGHEST) + br_ref[...]
    lane = lax.broadcasted_iota(I32, logits.shape, 1)
    cur = jnp.where(lane < n_experts, logits, NEG_INF)
    e_out = jnp.zeros(logits.shape, I32)
    g_out = jnp.zeros(logits.shape, F32)
    top = None
    picks = []
    for k in range(TOP_K):
        m = jnp.max(cur, axis=1, keepdims=True)
        idx = jnp.min(jnp.where(cur == m, lane, LANES), axis=1, keepdims=True)
        top = m if top is None else top
        e_out = jnp.where(lane == k, idx, e_out)
        g_out = jnp.where(lane == k, jnp.exp(m - top), g_out)
        picks.append(lane == idx)
        cur = jnp.where(picks[-1], NEG_INF, cur)
    e_ref[...] = e_out
    gate_ref[...] = g_out / jnp.sum(g_out, axis=1, keepdims=True)
    tm = logits.shape[0]
    chosen = jnp.where(picks[0] | picks[1] | picks[2] | picks[3], 1.0, 0.0)
    earlier = (lax.broadcasted_iota(I32, (tm, tm), 0) > lax.broadcasted_iota(I32, (tm, tm), 1))
    before = jnp.dot(jnp.where(earlier, 1.0, 0.0).astype(BF16), chosen.astype(BF16),
                     preferred_element_type=F32) + run_ref[0:1, :]
    r_out = jnp.zeros(logits.shape, I32)
    for k in range(TOP_K):
        rk = jnp.sum(jnp.where(picks[k], before, 0.0), axis=1, keepdims=True)
        r_out = jnp.where(lane == k, rk.astype(I32), r_out)
    rank_ref[...] = r_out
    run_ref[...] = run_ref[...] + jnp.sum(chosen, axis=0, keepdims=True)
    cnt_ref[...] = run_ref[...].astype(I32)


def _out_router(x2, ya, yb, wo, gm, g, sc, sh, wr, br, seq, tm=256):
    n, d = x2.shape
    tpb = seq // tm
    n_experts = wr.shape[1]
    wr_p = jnp.pad(wr, ((0, 0), (0, LANES - n_experts)))
    br_p = jnp.pad(br, (0, LANES - n_experts)).reshape(1, LANES)
    row = lambda w: pl.BlockSpec((tm, w), lambda i: (i, 0))
    full = lambda a: pl.BlockSpec(a.shape, lambda i: (0,) * a.ndim)
    per_b = pl.BlockSpec((1, 1, d), lambda i: (i // tpb, 0, 0))
    return pl.pallas_call(
        functools.partial(_out_router_kernel, n_experts=n_experts),
        grid=(n // tm,),
        in_specs=[row(d), row(D_A), row(D_B), full(wo), per_b, full(g), per_b, per_b, full(wr_p), full(br_p)],
        out_specs=[row(d), row(d), row(LANES), row(LANES), row(LANES),
                   pl.BlockSpec((SUBLANES, LANES), lambda i: (0, 0))],
        out_shape=[jax.ShapeDtypeStruct((n, d), F32), jax.ShapeDtypeStruct((n, d), F32),
                   jax.ShapeDtypeStruct((n, LANES), I32), jax.ShapeDtypeStruct((n, LANES), F32),
                   jax.ShapeDtypeStruct((n, LANES), I32), jax.ShapeDtypeStruct((SUBLANES, LANES), I32)],
        scratch_shapes=[pltpu.VMEM((SUBLANES, LANES), F32)],
        compiler_params=_cparams(("arbitrary",)),
        name="out_proj_router",
    )(x2, ya, yb, wo, gm, g, sc, sh, wr_p, br_p)


def _row_copy(src_hbm, row, dst_ref, slot, sem):
    return pltpu.make_async_copy(src_hbm.at[pl.ds(row, 1), :], dst_ref.at[pl.ds(slot, 1), :], sem)


def _moe_dispatch_kernel(dest_ref, pend_ref, h_ref, xs_hbm, zero_ref, zsem, sem, *, tm, n_experts):
    i = pl.program_id(0)

    @pl.when(i == 0)
    def _():
        zero_ref[...] = jnp.zeros(zero_ref.shape, zero_ref.dtype)

        def tail(e):
            start = pl.multiple_of(pend_ref[e + 1] - MOE_BLOCK, MOE_BLOCK)
            return pltpu.make_async_copy(zero_ref, xs_hbm.at[pl.ds(start, MOE_BLOCK), :], zsem)

        def has_slots(e):
            return pend_ref[e + 1] > pend_ref[e]

        def unused(j):
            start = pl.multiple_of(pend_ref[n_experts] + j * MOE_BLOCK, MOE_BLOCK)
            return pltpu.make_async_copy(zero_ref, xs_hbm.at[pl.ds(start, MOE_BLOCK), :], zsem)

        def is_unused(j):
            return pend_ref[n_experts] + j * MOE_BLOCK < xs_hbm.shape[0]
        for e in range(n_experts):
            pl.when(has_slots(e))(lambda e=e: tail(e).start())
        for j in range(n_experts):
            pl.when(is_unused(j))(lambda j=j: unused(j).start())
        for e in range(n_experts):
            pl.when(has_slots(e))(lambda e=e: tail(e).wait())
        for j in range(n_experts):
            pl.when(is_unused(j))(lambda j=j: unused(j).wait())

    base = i * tm * TOP_K

    def issue(t, carry):
        for k in range(TOP_K):
            pltpu.make_async_copy(h_ref.at[pl.ds(t, 1), :],
                                  xs_hbm.at[pl.ds(dest_ref[base + t * TOP_K + k], 1), :], sem).start(priority=k % 2)
        return carry
    lax.fori_loop(0, tm, issue, 0, unroll=2)
    for k in range(TOP_K):
        pltpu.make_async_copy(h_ref, xs_hbm.at[pl.ds(0, tm), :], sem).wait()


def _moe_dispatch(dest, pend0, h2, cap, tm=512):
    n, d = h2.shape
    n_experts = pend0.shape[0] - 1
    grid_spec = pltpu.PrefetchScalarGridSpec(
        num_scalar_prefetch=2, grid=(n // tm,),
        in_specs=[pl.BlockSpec((tm, d), lambda i, *_: (i, 0))],
        out_specs=pl.BlockSpec(memory_space=pl.ANY),
        scratch_shapes=[pltpu.VMEM((MOE_BLOCK, d), h2.dtype), pltpu.SemaphoreType.DMA, pltpu.SemaphoreType.DMA])
    return pl.pallas_call(
        functools.partial(_moe_dispatch_kernel, tm=tm, n_experts=n_experts), grid_spec=grid_spec,
        out_shape=jax.ShapeDtypeStruct((cap, d), h2.dtype),
        compiler_params=_cparams(("arbitrary",), disable_bounds_checks=True),
        name="moe_dispatch",
    )(dest, pend0, h2)


def _moe_expert_kernel(be_ref, nu_ref, xs_ref, w1_ref, b1_ref, w2_ref, b2_ref, y_ref, w1b_ref, w2b_ref, *, d_exp):
    i = pl.program_id(0)
    live = i < nu_ref[0]
    changed = jnp.logical_or(i == 0, be_ref[i] != be_ref[jnp.maximum(i - 1, 0)])

    @pl.when(live & changed)
    def _():
        w1b_ref[...] = w1_ref[0].astype(BF16)
        w2b_ref[...] = w2_ref[0].astype(BF16)

    @pl.when(live)
    def _():
        hid = jnp.dot(xs_ref[...].astype(BF16), w1b_ref[...], preferred_element_type=F32) + b1_ref[0]
        glu = jnp.minimum(hid[:, :d_exp], SWIGLU_LIMIT)
        lin = jnp.clip(hid[:, d_exp:], -SWIGLU_LIMIT, SWIGLU_LIMIT)
        act = glu * jax.nn.sigmoid(SWIGLU_ALPHA * glu) * (lin + 1.0)
        y_ref[...] = jnp.dot(act.astype(BF16), w2b_ref[...], preferred_element_type=F32) + b2_ref[0]

    @pl.when(jnp.logical_not(live))
    def _():
        y_ref[...] = jnp.zeros(y_ref.shape, y_ref.dtype)


def _moe_experts(block_e, n_used, xs, w1, b1, w2, b2):
    cap, d = xs.shape
    d2 = w1.shape[-1]
    d_exp = d2 // 2
    n_exp = w1.shape[0] * w1.shape[1]
    w1 = w1.reshape(n_exp, d, d2)
    w2 = w2.reshape(n_exp, d_exp, d)
    grid_spec = pltpu.PrefetchScalarGridSpec(
        num_scalar_prefetch=2, grid=(cap // MOE_BLOCK,),
        in_specs=[pl.BlockSpec((MOE_BLOCK, d), lambda i, be, nu: (jnp.minimum(i, nu[0] - 1), 0)),
                  pl.BlockSpec((1, d, d2), lambda i, be, nu: (be[i], 0, 0)),
                  pl.BlockSpec((1, 1, d2), lambda i, be, nu: (be[i], 0, 0)),
                  pl.BlockSpec((1, d_exp, d), lambda i, be, nu: (be[i], 0, 0)),
                  pl.BlockSpec((1, 1, d), lambda i, be, nu: (be[i], 0, 0))],
        out_specs=pl.BlockSpec((MOE_BLOCK, d), lambda i, be, nu: (i, 0)),
        scratch_shapes=[pltpu.VMEM((d, d2), BF16), pltpu.VMEM((d_exp, d), BF16)])
    return pl.pallas_call(
        functools.partial(_moe_expert_kernel, d_exp=d_exp), grid_spec=grid_spec,
        out_shape=jax.ShapeDtypeStruct((cap, d), F32),
        compiler_params=_cparams(("arbitrary",)),
        name="moe_experts",
    )(block_e, n_used, xs, w1, b1.reshape(n_exp, 1, d2), w2, b2.reshape(n_exp, 1, d))


def _moe_combine_kernel(dest_ref, ys_hbm, x1_ref, gate_ref, gf_ref, gfin_ref, o_ref, rows_ref, sem, *, tm, final):
    base = pl.program_id(0) * tm * TOP_K

    def issue(t, carry):
        for k in range(TOP_K):
            _row_copy(ys_hbm, dest_ref[base + t * TOP_K + k], rows_ref.at[k], t, sem).start()
        return carry
    lax.fori_loop(0, tm, issue, 0, unroll=2)
    for k in range(TOP_K):
        pltpu.make_async_copy(ys_hbm.at[pl.ds(0, tm), :], rows_ref.at[k], sem).wait()
    moe = gate_ref[:, 0:1] * rows_ref[0]
    for k in range(1, TOP_K):
        moe = moe + gate_ref[:, k:k + 1] * rows_ref[k]
    out = x1_ref[...] + gf_ref[0] * moe
    if final:
        out = _rms(out, gfin_ref[...])
    o_ref[...] = out


def _moe_combine(dest, ys, x1, gate, gf, g_final, seq, final, tm=256):
    n, d = x1.shape
    tpb = seq // tm
    grid_spec = pltpu.PrefetchScalarGridSpec(
        num_scalar_prefetch=1, grid=(n // tm,),
        in_specs=[pl.BlockSpec(memory_space=pl.ANY),
                  pl.BlockSpec((tm, d), lambda i, *_: (i, 0)),
                  pl.BlockSpec((tm, LANES), lambda i, *_: (i, 0)),
                  pl.BlockSpec((1, 1, d), lambda i, *_: (i // tpb, 0, 0)),
                  pl.BlockSpec((1, d), lambda i, *_: (0, 0))],
        out_specs=pl.BlockSpec((tm, d), lambda i, *_: (i, 0)),
        scratch_shapes=[pltpu.VMEM((TOP_K, tm, d), F32), pltpu.SemaphoreType.DMA])
    return pl.pallas_call(
        functools.partial(_moe_combine_kernel, tm=tm, final=final), grid_spec=grid_spec,
        out_shape=jax.ShapeDtypeStruct((n, d), F32),
        compiler_params=_cparams(("arbitrary",), disable_bounds_checks=True),
        name="moe_combine",
    )(dest, ys, x1, gate, gf, g_final)


def _route_tables(top_e, rank, counts, n_blocks):
    n_experts = counts.shape[0]
    padded = (counts + MOE_BLOCK - 1) // MOE_BLOCK * MOE_BLOCK
    pend = jnp.cumsum(padded)
    pstart = pend - padded
    onehot = top_e[..., None] == jnp.arange(n_experts, dtype=I32)
    dest = (jnp.sum(jnp.where(onehot, pstart, 0), axis=-1) + rank).reshape(-1).astype(I32)
    block_e = jnp.minimum(jnp.sum(pend[None, :] <= jnp.arange(n_blocks, dtype=I32)[:, None] * MOE_BLOCK, axis=1),
                          n_experts - 1).astype(I32)
    n_used = (pend[-1:] // MOE_BLOCK).astype(I32)
    pend0 = jnp.concatenate([jnp.zeros((1,), I32), pend.astype(I32)])
    return dest, pend0, block_e, n_used


def kernel(x, c, positions, rel_bias, norm_mix, w_ada, b_ada, w_in, q_norm, w_uq, kv_norm, w_ukv, w_out, norm_ffn, w_router, b_router, w1, b1, w2, b2, norm_final):
    bsz, seq, d = x.shape
    depth = w_ada.shape[0]
    n = bsz * seq
    n_experts = w_router.shape[-1]
    mod = _ada_mod(c, w_ada, b_ada)
    ctab, stab = _rope_tables(positions)
    row = lambda v: v.reshape(1, -1)
    x2 = x.reshape(n, d)
    for l in range(depth):
        sh_m, sc_m, g_m, sh_f, sc_f, g_f = [mod[l, :, i * d:(i + 1) * d].reshape(bsz, 1, d) for i in range(6)]
        wuq, wuqs = _pack_uq(w_uq[l])
        wukk, wukv = _pack_ukv(w_ukv[l])
        qa, ka, va, qi, ki, wi, qb, kb, vb = _proj(
            x2, row(norm_mix[l]), sc_m, sh_m, _pack_in_weights(w_in[l]), row(q_norm[l]), row(kv_norm[l]),
            wuq, wuqs, wukk, wukv, ctab, stab, seq)
        ya = _attn_a(qa, qi, wi, ka, va, ki, positions, rel_bias)
        yb = _attn_b(qb, kb, vb, positions)
        x1, h2, top_e, gate, rank, counts = _out_router(x2, ya, yb, w_out[l].astype(BF16), g_m, row(norm_ffn[l]),
                                                        sc_f, sh_f, w_router[l], b_router[l], seq)
        n_blocks = -(-n * TOP_K // MOE_BLOCK) + n_experts
        dest, pend0, block_e, n_used = _route_tables(top_e[:, :TOP_K], rank[:, :TOP_K], counts[0, :n_experts],
                                                     n_blocks)
        xs = _moe_dispatch(dest, pend0, h2, n_blocks * MOE_BLOCK)
        ys = _moe_experts(block_e + l * n_experts, n_used, xs, w1, b1, w2, b2)
        x2 = _moe_combine(dest, ys, x1, gate, g_f, row(norm_final), seq, final=(l == depth - 1))
    return x2.reshape(bsz, seq, d)
```

```python
import functools
import math

import numpy as np
import jax
import jax.numpy as jnp
from jax import lax
from jax.experimental import pallas as pl
from jax.experimental.pallas import tpu as pltpu

CHUNK = 64
EPS = 1e-6
A_HEADS = 8
A_HEAD_DIM = 64
IDX_HEADS = 8
IDX_DIM = 64
TOPK_MAX = 256
B_HEADS = 8
Q_LORA = 256
KV_LORA = 128
QK_NOPE = 64
QK_ROPE = 32
V_DIM = 64
ROPE_THETA = 10000.0
N_BUCKETS = 32
MAX_DISTANCE = 128
TOP_K = 4
SWIGLU_LIMIT = 7.0
SWIGLU_ALPHA = 1.702
MOE_BLOCK = 256

LANES = 128
SUBLANES = 8
VMEM_LIMIT = 56 * 1024 * 1024
INT_MIN = -2 ** 31
NEG_INF = float("-inf")
BIAS_ROWS = 32

F32 = jnp.float32
BF16 = jnp.bfloat16
I32 = jnp.int32

_T5_EXACT = (N_BUCKETS // 2) // 2
_T5_THRESH = tuple(
    int(math.ceil(_T5_EXACT * (MAX_DISTANCE / _T5_EXACT) ** (k / (N_BUCKETS // 2 - _T5_EXACT)) - 1e-9))
    for k in range(1, N_BUCKETS // 2 - _T5_EXACT))
_T5_FAR = _T5_THRESH[-1]


def _cparams(sem, vmem=VMEM_LIMIT, **kw):
    return pltpu.CompilerParams(dimension_semantics=sem, vmem_limit_bytes=vmem, **kw)


def _dot_t(a, b):
    return lax.dot_general(a, b, (((1,), (1,)), ((), ())), preferred_element_type=F32)


def _ada_kernel(c_ref, w_ref, b_ref, o_ref):
    c = c_ref[...]
    cond = c * jax.nn.sigmoid(c)
    o_ref[0] = jnp.dot(cond, w_ref[0], preferred_element_type=F32,
                       precision=lax.Precision.HIGHEST) + b_ref[0]


def _ada_mod(c, w_ada, b_ada, tn=512):
    depth, d, n6 = w_ada.shape
    b = c.shape[0]
    return pl.pallas_call(
        _ada_kernel,
        grid=(depth, n6 // tn),
        in_specs=[pl.BlockSpec((b, d), lambda l, j: (0, 0)),
                  pl.BlockSpec((1, d, tn), lambda l, j: (l, 0, j)),
                  pl.BlockSpec((1, 1, tn), lambda l, j: (l, 0, j))],
        out_specs=pl.BlockSpec((1, b, tn), lambda l, j: (l, 0, j)),
        out_shape=jax.ShapeDtypeStruct((depth, b, n6), F32),
        compiler_params=_cparams(("arbitrary", "arbitrary")),
        name="ada_mod",
    )(c, w_ada, b_ada.reshape(depth, 1, n6))


D_A = A_HEADS * A_HEAD_DIM
D_B = B_HEADS * V_DIM
HB = 128
_C_QA, _C_KA, _C_VA, _C_QI = 0, D_A, 2 * D_A, 3 * D_A
_C_KI = 4 * D_A
_C_WI = _C_KI + LANES
_C_CQ = _C_WI + LANES
_C_CKV = _C_CQ + Q_LORA
_C_KR = _C_CKV + KV_LORA
_C_KRS = _C_KR + LANES
_C_END = _C_KRS + LANES


def _pack_in_weights(w_in):
    d = w_in.shape[0]
    offs = np.cumsum((0, D_A, D_A, D_A, IDX_HEADS * IDX_DIM, IDX_DIM, IDX_HEADS, Q_LORA, KV_LORA, QK_ROPE))
    seg = [w_in[:, offs[i]:offs[i + 1]] for i in range(9)]
    q_a, k_a, v_a, q_i, k_i, w_i, c_q, c_kv, k_r = seg
    z = lambda n: jnp.zeros((d, n), w_in.dtype)
    half = QK_ROPE // 2
    k_rs = jnp.concatenate([k_r[:, half:], k_r[:, :half]], axis=1)
    cols = [q_a * (A_HEAD_DIM ** -0.5), k_a, v_a, q_i, k_i, k_i, w_i, z(LANES - IDX_HEADS), c_q, c_kv,
            z(QK_NOPE), k_r, z(HB - QK_NOPE - QK_ROPE), z(QK_NOPE), k_rs, z(HB - QK_NOPE - QK_ROPE)]
    return jnp.concatenate(cols, axis=1).astype(BF16)


def _pack_uq(w_uq):
    r = w_uq.shape[0]
    w = w_uq.reshape(r, B_HEADS, QK_NOPE + QK_ROPE)
    nope, rope = w[..., :QK_NOPE], w[..., QK_NOPE:]
    half = QK_ROPE // 2
    z = jnp.zeros((r, B_HEADS, HB - QK_NOPE - QK_ROPE), w.dtype)
    main = jnp.concatenate([nope, rope, z], axis=-1).reshape(r, B_HEADS * HB)
    swap = jnp.concatenate([jnp.zeros_like(nope), rope[..., half:], rope[..., :half], z], axis=-1)
    return main.astype(BF16), swap.reshape(r, B_HEADS * HB).astype(BF16)


def _pack_ukv(w_ukv):
    r = w_ukv.shape[0]
    w = w_ukv.reshape(r, B_HEADS, QK_NOPE + V_DIM)
    k = jnp.concatenate([w[..., :QK_NOPE], jnp.zeros((r, B_HEADS, HB - QK_NOPE), w.dtype)], axis=-1)
    v = w[..., QK_NOPE:]
    return k.reshape(r, B_HEADS * HB).astype(BF16), v.reshape(r, B_HEADS * V_DIM).astype(BF16)


def _rope_tables(positions):
    half = QK_ROPE // 2
    inv = ROPE_THETA ** (-jnp.arange(half, dtype=F32) / half)
    ang = positions.astype(F32).reshape(-1, 1) * inv[None, :]
    cos, sin = jnp.cos(ang), jnp.sin(ang)
    n = ang.shape[0]
    one = jnp.ones((n, QK_NOPE), F32)
    z = lambda k: jnp.zeros((n, k), F32)
    ctab = jnp.concatenate([one, cos, cos, z(HB - QK_NOPE - QK_ROPE)], axis=1)
    stab = jnp.concatenate([z(QK_NOPE), -sin, sin, z(HB - QK_NOPE - QK_ROPE)], axis=1)
    return ctab, stab


def _rms(x, g):
    return x * lax.rsqrt(jnp.mean(x * x, axis=-1, keepdims=True) + EPS) * g


def _proj_kernel(x_ref, g_ref, sc_ref, sh_ref, wp_ref, qn_ref, kvn_ref, wuq_ref, wuqs_ref, wukk_ref, wukv_ref,
                 ct_ref, st_ref,
                 qa_ref, ka_ref, va_ref, qi_ref, ki_ref, wi_ref, qb_ref, kb_ref, vb_ref):
    x = x_ref[...]
    h = _rms(x, g_ref[...]) * (1.0 + sc_ref[0]) + sh_ref[0]
    hb = h.astype(BF16)
    seg = lambda a, b: jnp.dot(hb, wp_ref[:, a:b], preferred_element_type=F32)
    qa_ref[...] = seg(_C_QA, _C_KA).astype(BF16)
    ka_ref[...] = seg(_C_KA, _C_VA).astype(BF16)
    va_ref[...] = seg(_C_VA, _C_QI).astype(BF16)
    qi_ref[...] = seg(_C_QI, _C_KI).astype(BF16)
    ki_ref[...] = seg(_C_KI, _C_WI).astype(BF16)
    wi_ref[...] = seg(_C_WI, _C_CQ)
    ct = ct_ref[...]
    st = st_ref[...]
    cq = _rms(seg(_C_CQ, _C_CKV), qn_ref[...]).astype(BF16)
    scale = (QK_NOPE + QK_ROPE) ** -0.5
    for hd in range(B_HEADS):
        sl = slice(hd * HB, (hd + 1) * HB)
        q = jnp.dot(cq, wuq_ref[:, sl], preferred_element_type=F32)
        qs = jnp.dot(cq, wuqs_ref[:, sl], preferred_element_type=F32)
        qb_ref[:, sl] = ((q * ct + qs * st) * scale).astype(BF16)
    ckv = _rms(seg(_C_CKV, _C_KR), kvn_ref[...]).astype(BF16)
    kr = seg(_C_KR, _C_KRS) * ct + seg(_C_KRS, _C_END) * st
    for hd in range(B_HEADS):
        sl = slice(hd * HB, (hd + 1) * HB)
        kb_ref[:, sl] = (jnp.dot(ckv, wukk_ref[:, sl], preferred_element_type=F32) + kr).astype(BF16)
    vb_ref[...] = jnp.dot(ckv, wukv_ref[...], preferred_element_type=F32).astype(BF16)


def _proj(x2, g, sc, sh, wp, qn, kvn, wuq, wuqs, wukk, wukv, ctab, stab, seq, tm=256):
    n, d = x2.shape
    tpb = seq // tm
    row = lambda w: pl.BlockSpec((tm, w), lambda i: (i, 0))
    full = lambda a: pl.BlockSpec(a.shape, lambda i: (0,) * a.ndim)
    per_b = pl.BlockSpec((1, 1, d), lambda i: (i // tpb, 0, 0))
    outs = [(D_A, BF16), (D_A, BF16), (D_A, BF16), (D_A, BF16), (LANES, BF16), (LANES, F32),
            (B_HEADS * HB, BF16), (B_HEADS * HB, BF16), (D_B, BF16)]
    return pl.pallas_call(
        _proj_kernel,
        grid=(n // tm,),
        in_specs=[row(d), full(g), per_b, per_b, full(wp), full(qn), full(kvn), full(wuq), full(wuqs),
                  full(wukk), full(wukv), row(HB), row(HB)],
        out_specs=[row(w) for w, _ in outs],
        out_shape=[jax.ShapeDtypeStruct((n, w), dt) for w, dt in outs],
        compiler_params=_cparams(("arbitrary",)),
        name="in_proj",
    )(x2, g, sc, sh, wp, qn, kvn, wuq, wuqs, wukk, wukv, ctab, stab)


def _pair_mask(x_pair, odd):
    lane = lax.broadcasted_iota(I32, x_pair.shape, 1)
    keep = (lane >= A_HEAD_DIM) if odd else (lane < A_HEAD_DIM)
    return jnp.where(keep, x_pair, jnp.zeros_like(x_pair))


def _flash_step(hd, s, v_pair, m_ref, l_ref, acc_ref):
    m_prev = m_ref[hd]
    m_new = jnp.maximum(m_prev, jnp.max(s, axis=1, keepdims=True))
    m_safe = jnp.where(m_new == NEG_INF, 0.0, m_new)
    alpha = jnp.exp(m_prev - m_safe)
    p = jnp.exp(s - jnp.concatenate([m_safe] * (s.shape[1] // LANES), axis=1))
    l_ref[hd] = alpha * l_ref[hd] + jnp.sum(p, axis=1, keepdims=True)
    acc_ref[hd] = alpha * acc_ref[hd] + jnp.dot(p.astype(BF16), v_pair, preferred_element_type=F32)
    m_ref[hd] = m_new


def _flash_init(m_ref, l_ref, acc_ref):
    m_ref[...] = jnp.full(m_ref.shape, NEG_INF, F32)
    l_ref[...] = jnp.zeros(l_ref.shape, F32)
    acc_ref[...] = jnp.zeros(acc_ref.shape, F32)


def _flash_finish(y_ref, l_ref, acc_ref, n_heads):
    lane = lax.broadcasted_iota(I32, acc_ref.shape[1:], 1)
    for j in range(n_heads // 2):
        even = acc_ref[2 * j] / l_ref[2 * j]
        odd = acc_ref[2 * j + 1] / l_ref[2 * j + 1]
        y_ref[:, j * LANES:(j + 1) * LANES] = jnp.where(lane < V_DIM, even, odd).astype(y_ref.dtype)


def _t5_bias_heads(rel, bias_ref, lo, hi):
    nb = N_BUCKETS // 2
    n = jnp.abs(rel)
    bucket = jnp.minimum(n, _T5_EXACT)
    for t in _T5_THRESH:
        bucket = bucket + (n >= t).astype(I32)
    if hi >= nb:
        bucket = bucket + (nb if lo >= nb else jnp.where(rel > 0, nb, 0))
    outs = [jnp.full(rel.shape, bias_ref[lo * A_HEADS + hd], F32) for hd in range(A_HEADS)]
    for j in range(lo + 1, hi + 1):
        hit = bucket == j
        outs = [jnp.where(hit, bias_ref[j * A_HEADS + hd], outs[hd]) for hd in range(A_HEADS)]
    return outs


def _attn_a_kernel(nk_ref, qlo_ref, qhi_ref, klo_ref, khi_ref,
                   bias_ref, qa_ref, qi_ref, wi_ref, ka_ref, va_ref, ki_ref, pq_ref, pk_ref,
                   y_ref,
                   key_ref, keyt_ref, am_ref, ex_ref, j_ref, wb_ref, m_ref, l_ref, acc_ref,
                   *, topk, tq, tk, nq, nkt):
    b = pl.program_id(0)
    qi_idx = pl.program_id(1)
    nk = nk_ref[b * nq + qi_idx]
    pq = pq_ref[...]
    qchunk = jnp.right_shift(pq, int(math.log2(CHUNK)))
    lane_tk = lax.broadcasted_iota(I32, (tq, tk), 1)

    wide = lambda v: jnp.concatenate([v] * (tk // LANES), axis=1)
    for hd in range(IDX_HEADS):
        wb_ref[hd] = jnp.broadcast_to(wi_ref[:, hd:hd + 1], (tq, LANES))

    def score_body(kt, carry):
        ks = ki_ref[pl.ds(pl.multiple_of(kt * tk, tk), tk), :]
        sc = jnp.zeros((tq, tk), F32)
        for hd in range(IDX_HEADS):
            pair = qi_ref[:, (hd // 2) * LANES:(hd // 2 + 1) * LANES]
            r = _dot_t(_pair_mask(pair, hd % 2), ks)
            sc = sc + wide(wb_ref[hd]) * jnp.maximum(r, 0.0)
        bits = pltpu.bitcast(sc, I32)
        key = bits ^ (jnp.right_shift(bits, 31) & jnp.int32(0x7FFFFFFF))
        kchunk = jnp.right_shift(pk_ref[kt], int(math.log2(CHUNK)))
        key = jnp.where(kchunk <= qchunk, key, jnp.int32(INT_MIN))
        key_ref[kt] = key
        keyt_ref[kt] = key.T
        return carry
    lax.fori_loop(0, nk, score_body, 0)

    groups = tk // SUBLANES
    sub_idx = (lax.broadcasted_iota(I32, (groups, SUBLANES, tq), 0) * SUBLANES
               + lax.broadcasted_iota(I32, (groups, SUBLANES, tq), 1))

    def count(pred):
        def body(kt, acc):
            k3 = keyt_ref[kt].reshape(groups, SUBLANES, tq)
            return acc + jnp.sum(jnp.where(pred(k3, kt * tk), 1, 0), axis=0)
        acc = lax.fori_loop(0, nk, body, jnp.zeros((SUBLANES, tq), I32))
        return jnp.broadcast_to(jnp.sum(acc, axis=0, keepdims=True), (SUBLANES, tq))

    def bit_body(i, t):
        cand = t + jnp.left_shift(jnp.int32(1), 31 - i)
        c = count(lambda k, k0: k >= cand)
        return jnp.where(c >= topk, cand, t)
    thr_t = lax.fori_loop(0, 32, bit_body, jnp.full((SUBLANES, tq), INT_MIN, I32))
    need = topk - count(lambda k, k0: k > thr_t)
    n_eq = count(lambda k, k0: k == thr_t)
    real = thr_t > jnp.int32(INT_MIN)
    j_ref[...] = jnp.where(real, jnp.int32(nkt * tk), jnp.int32(-1))
    tie = jnp.max(jnp.where(real & (n_eq > need), 1, 0))

    @pl.when(tie > 0)
    def _():
        def idx_body(i, p):
            cand = p + jnp.left_shift(jnp.int32(1), (nkt * tk).bit_length() - 2 - i)
            c = count(lambda k, k0: (k == thr_t) & (sub_idx + k0 < cand))
            return jnp.where(c < need, cand, p)
        p = lax.fori_loop(0, (nkt * tk).bit_length() - 1, idx_body, jnp.zeros((SUBLANES, tq), I32))
        j_ref[...] = jnp.where(real & (n_eq > need), p, j_ref[...])

    to_rows = lambda v: jnp.broadcast_to(v[:1], (LANES, tq)).T
    thr_w = wide(to_rows(thr_t))
    jsel_w = wide(to_rows(j_ref[...]))

    def mask_body(kt, carry):
        k = key_ref[kt]
        sel = (k > thr_w) | ((k == thr_w) & (lane_tk + kt * tk <= jsel_w))
        am_ref[kt] = jnp.where(sel, 0.0, NEG_INF)
        return carry
    lax.fori_loop(0, nk, mask_body, 0)

    _flash_init(m_ref, l_ref, acc_ref)
    q_lo = qlo_ref[b * nq + qi_idx]
    q_hi = qhi_ref[b * nq + qi_idx]

    def bias_chunk(kt, c):
        cs = slice(c * LANES, (c + 1) * LANES)
        chunk = b * (nkt * tk // LANES) + kt * (tk // LANES) + c
        shift = int(math.log2(CHUNK))
        far_past = khi_ref[chunk] - q_lo <= -_T5_FAR
        far_future = klo_ref[chunk] - q_hi >= _T5_FAR

        @pl.when(far_past)
        def _():
            for hd in range(A_HEADS):
                ex_ref[hd, :, cs] = am_ref[kt, :, cs] + bias_ref[(N_BUCKETS // 2 - 1) * A_HEADS + hd]

        @pl.when(far_future)
        def _():
            for hd in range(A_HEADS):
                ex_ref[hd, :, cs] = am_ref[kt, :, cs] + bias_ref[(N_BUCKETS - 1) * A_HEADS + hd]

        def elementwise(lo, hi):
            pkc = pk_ref[kt][:, cs]

            def rows_body(r, carry):
                rows = pl.ds(pl.multiple_of(r * BIAS_ROWS, BIAS_ROWS), BIAS_ROWS)
                am = am_ref[kt, rows, cs]
                for hd, bias in enumerate(_t5_bias_heads(pkc - pq_ref[rows, :], bias_ref, lo, hi)):
                    ex_ref[hd, rows, cs] = am + bias
                return carry
            lax.fori_loop(0, tq // BIAS_ROWS, rows_body, 0)

        near = jnp.logical_not(far_past | far_future)
        hidden = jnp.right_shift(klo_ref[chunk], shift) > jnp.right_shift(q_hi, shift)
        behind = khi_ref[chunk] <= q_lo
        ahead = klo_ref[chunk] > q_hi

        @pl.when(near & hidden)
        def _():
            for hd in range(A_HEADS):
                ex_ref[hd, :, cs] = am_ref[kt, :, cs]

        pl.when(near & behind)(lambda: elementwise(0, N_BUCKETS // 2 - 1))
        pl.when(near & ahead & jnp.logical_not(hidden))(lambda: elementwise(N_BUCKETS // 2, N_BUCKETS - 1))
        pl.when(near & jnp.logical_not(behind | ahead))(lambda: elementwise(0, N_BUCKETS - 1))

    def attn_body(kt, carry):
        start = pl.multiple_of(kt * tk, tk)
        for c in range(tk // LANES):
            bias_chunk(kt, c)

        for hd in range(A_HEADS):
            pr = slice((hd // 2) * LANES, (hd // 2 + 1) * LANES)
            qm = _pair_mask(qa_ref[:, pr], hd % 2)
            s = _dot_t(qm, ka_ref[pl.ds(start, tk), pr]) + ex_ref[hd]
            _flash_step(hd, s, va_ref[pl.ds(start, tk), pr], m_ref, l_ref, acc_ref)
        return carry
    lax.fori_loop(0, nk, attn_body, 0)
    _flash_finish(y_ref, l_ref, acc_ref, A_HEADS)


def _tile_tables(positions, tq, tk):
    b, s = positions.shape
    pq = positions.reshape(b, s // tq, tq)
    pk = positions.reshape(b, s // tk, tk)
    pc = positions.reshape(b, s // LANES, LANES)
    q_lo, q_hi = pq.min(-1), pq.max(-1)
    vis = (pk.min(-1) // CHUNK)[:, None, :] <= (q_hi // CHUNK)[:, :, None]
    last = jnp.max(jnp.where(vis, jnp.arange(s // tk, dtype=I32)[None, None, :] + 1, 0), axis=-1)
    flat = lambda a: a.reshape(-1).astype(I32)
    return flat(last), flat(q_lo), flat(q_hi), flat(pc.min(-1)), flat(pc.max(-1))


def _attn_a(qa, qi, wi, ka, va, ki, positions, rel_bias, tq=256, tk=512):
    b, s = positions.shape
    n = b * s
    nq, nkt = s // tq, s // tk
    topk = min(TOPK_MAX, s // 4)
    tables = _tile_tables(positions, tq, tk)
    pos_col = positions.reshape(n, 1)
    pos_row = positions.reshape(b, nkt, 1, tk)
    qrow = lambda w: pl.BlockSpec((tq, w), lambda bi, i, *_: (bi * nq + i, 0))
    kfull = lambda w: pl.BlockSpec((s, w), lambda bi, i, *_: (bi, 0))
    kern = functools.partial(_attn_a_kernel, topk=topk, tq=tq, tk=tk, nq=nq, nkt=nkt)
    grid_spec = pltpu.PrefetchScalarGridSpec(
        num_scalar_prefetch=5,
        grid=(b, nq),
        in_specs=[pl.BlockSpec(memory_space=pltpu.SMEM),
                  qrow(D_A), qrow(D_A), qrow(LANES), kfull(D_A), kfull(D_A), kfull(LANES),
                  pl.BlockSpec((tq, 1), lambda bi, i, *_: (bi * nq + i, 0)),
                  pl.BlockSpec((None, nkt, 1, tk), lambda bi, i, *_: (bi, 0, 0, 0))],
        out_specs=qrow(D_A),
        scratch_shapes=[pltpu.VMEM((nkt, tq, tk), I32), pltpu.VMEM((nkt, tk, tq), I32),
                        pltpu.VMEM((nkt, tq, tk), F32),
                        pltpu.VMEM((A_HEADS, tq, tk), F32), pltpu.VMEM((SUBLANES, tq), I32),
                        pltpu.VMEM((IDX_HEADS, tq, LANES), F32),
                        pltpu.VMEM((A_HEADS, tq, LANES), F32), pltpu.VMEM((A_HEADS, tq, LANES), F32),
                        pltpu.VMEM((A_HEADS, tq, LANES), F32)])
    return pl.pallas_call(
        kern, grid_spec=grid_spec,
        out_shape=jax.ShapeDtypeStruct((n, D_A), BF16),
        compiler_params=_cparams(("arbitrary", "arbitrary")),
        name="attn_indexer",
    )(*tables, rel_bias.reshape(-1), qa, qi, wi, ka, va, ki, pos_col, pos_row)


def _attn_b_kernel(nk_ref, qb_ref, kb_ref, vb_ref, pq_ref, pk_ref, y_ref, m_ref, l_ref, acc_ref, *, tq, tk, nq):
    b = pl.program_id(0)
    nk = nk_ref[b * nq + pl.program_id(1)]
    qchunk = jnp.right_shift(pq_ref[...], int(math.log2(CHUNK)))
    _flash_init(m_ref, l_ref, acc_ref)

    def body(kt, carry):
        start = pl.multiple_of(kt * tk, tk)
        kchunk = jnp.right_shift(pk_ref[kt], int(math.log2(CHUNK)))
        am = jnp.where(kchunk <= qchunk, 0.0, NEG_INF)
        for hd in range(B_HEADS):
            hs = slice(hd * HB, (hd + 1) * HB)
            pr = slice((hd // 2) * LANES, (hd // 2 + 1) * LANES)
            s = _dot_t(qb_ref[:, hs], kb_ref[pl.ds(start, tk), hs]) + am
            _flash_step(hd, s, vb_ref[pl.ds(start, tk), pr], m_ref, l_ref, acc_ref)
        return carry
    lax.fori_loop(0, nk, body, 0)
    _flash_finish(y_ref, l_ref, acc_ref, B_HEADS)


def _attn_b(qb, kb, vb, positions, tq=256, tk=512):
    b, s = positions.shape
    n = b * s
    nq, nkt = s // tq, s // tk
    nk = _tile_tables(positions, tq, tk)[0]
    qrow = lambda w: pl.BlockSpec((tq, w), lambda bi, i, *_: (bi * nq + i, 0))
    kfull = lambda w: pl.BlockSpec((s, w), lambda bi, i, *_: (bi, 0))
    grid_spec = pltpu.PrefetchScalarGridSpec(
        num_scalar_prefetch=1,
        grid=(b, nq),
        in_specs=[qrow(B_HEADS * HB), kfull(B_HEADS * HB), kfull(D_B),
                  pl.BlockSpec((tq, 1), lambda bi, i, *_: (bi * nq + i, 0)),
                  pl.BlockSpec((None, nkt, 1, tk), lambda bi, i, *_: (bi, 0, 0, 0))],
        out_specs=qrow(D_B),
        scratch_shapes=[pltpu.VMEM((B_HEADS, tq, LANES), F32)] * 3)
    return pl.pallas_call(
        functools.partial(_attn_b_kernel, tq=tq, tk=tk, nq=nq), grid_spec=grid_spec,
        out_shape=jax.ShapeDtypeStruct((n, D_B), BF16),
        compiler_params=_cparams(("arbitrary", "arbitrary")),
        name="attn_latent",
    )(nk, qb, kb, vb, positions.reshape(n, 1), positions.reshape(b, nkt, 1, tk))


def _out_router_kernel(x_ref, ya_ref, yb_ref, wo_ref, gm_ref, g_ref, sc_ref, sh_ref, wr_ref, br_ref,
                       x1_ref, h2_ref, e_ref, gate_ref, rank_ref, cnt_ref, run_ref, *, n_experts):
    @pl.when(pl.program_id(0) == 0)
    def _():
        run_ref[...] = jnp.zeros(run_ref.shape, F32)

    mix = (jnp.dot(ya_ref[...], wo_ref[:D_A, :], preferred_element_type=F32)
           + jnp.dot(yb_ref[...], wo_ref[D_A:, :], preferred_element_type=F32))
    x1 = x_ref[...] + gm_ref[0] * mix
    x1_ref[...] = x1
    h2 = _rms(x1, g_ref[...]) * (1.0 + sc_ref[0]) + sh_ref[0]
    h2_ref[...] = h2
    logits = jnp.dot(h2, wr_ref[...], preferred_element_type=F32, precision=lax.Precision.HIGHEST) + br_ref[...]
    lane = lax.broadcasted_iota(I32, logits.shape, 1)
    cur = jnp.where(lane < n_experts, logits, NEG_INF)
    e_out = jnp.zeros(logits.shape, I32)
    g_out = jnp.zeros(logits.shape, F32)
    top = None
    picks = []
    for k in range(TOP_K):
        m = jnp.max(cur, axis=1, keepdims=True)
        idx = jnp.min(jnp.where(cur == m, lane, LANES), axis=1, keepdims=True)
        top = m if top is None else top
        e_out = jnp.where(lane == k, idx, e_out)
        g_out = jnp.where(lane == k, jnp.exp(m - top), g_out)
        picks.append(lane == idx)
        cur = jnp.where(picks[-1], NEG_INF, cur)
    e_ref[...] = e_out
    gate_ref[...] = g_out / jnp.sum(g_out, axis=1, keepdims=True)
    tm = logits.shape[0]
    chosen = jnp.where(picks[0] | picks[1] | picks[2] | picks[3], 1.0, 0.0)
    earlier = (lax.broadcasted_iota(I32, (tm, tm), 0) > lax.broadcasted_iota(I32, (tm, tm), 1))
    before = jnp.dot(jnp.where(earlier, 1.0, 0.0).astype(BF16), chosen.astype(BF16),
                     preferred_element_type=F32) + run_ref[0:1, :]
    r_out = jnp.zeros(logits.shape, I32)
    for k in range(TOP_K):
        rk = jnp.sum(jnp.where(picks[k], before, 0.0), axis=1, keepdims=True)
        r_out = jnp.where(lane == k, rk.astype(I32), r_out)
    rank_ref[...] = r_out
    run_ref[...] = run_ref[...] + jnp.sum(chosen, axis=0, keepdims=True)
    cnt_ref[...] = run_ref[...].astype(I32)


def _out_router(x2, ya, yb, wo, gm, g, sc, sh, wr, br, seq, tm=512):
    n, d = x2.shape
    tpb = seq // tm
    n_experts = wr.shape[1]
    wr_p = jnp.pad(wr, ((0, 0), (0, LANES - n_experts)))
    br_p = jnp.pad(br, (0, LANES - n_experts)).reshape(1, LANES)
    row = lambda w: pl.BlockSpec((tm, w), lambda i: (i, 0))
    full = lambda a: pl.BlockSpec(a.shape, lambda i: (0,) * a.ndim)
    per_b = pl.BlockSpec((1, 1, d), lambda i: (i // tpb, 0, 0))
    return pl.pallas_call(
        functools.partial(_out_router_kernel, n_experts=n_experts),
        grid=(n // tm,),
        in_specs=[row(d), row(D_A), row(D_B), full(wo), per_b, full(g), per_b, per_b, full(wr_p), full(br_p)],
        out_specs=[row(d), row(d), row(LANES), row(LANES), row(LANES),
                   pl.BlockSpec((SUBLANES, LANES), lambda i: (0, 0))],
        out_shape=[jax.ShapeDtypeStruct((n, d), F32), jax.ShapeDtypeStruct((n, d), F32),
                   jax.ShapeDtypeStruct((n, LANES), I32), jax.ShapeDtypeStruct((n, LANES), F32),
                   jax.ShapeDtypeStruct((n, LANES), I32), jax.ShapeDtypeStruct((SUBLANES, LANES), I32)],
        scratch_shapes=[pltpu.VMEM((SUBLANES, LANES), F32)],
        compiler_params=_cparams(("arbitrary",)),
        name="out_proj_router",
    )(x2, ya, yb, wo, gm, g, sc, sh, wr_p, br_p)


def _row_copy(src_hbm, row, dst_ref, slot, sem):
    return pltpu.make_async_copy(src_hbm.at[pl.ds(row, 1), :], dst_ref.at[pl.ds(slot, 1), :], sem)


def _moe_dispatch_kernel(dest_ref, pend_ref, h_ref, xs_hbm, zero_ref, zsem, sem, *, tm, n_experts):
    i = pl.program_id(0)

    @pl.when(i == 0)
    def _():
        zero_ref[...] = jnp.zeros(zero_ref.shape, zero_ref.dtype)

        def tail(e):
            start = pl.multiple_of(pend_ref[e + 1] - MOE_BLOCK, MOE_BLOCK)
            return pltpu.make_async_copy(zero_ref, xs_hbm.at[pl.ds(start, MOE_BLOCK), :], zsem)

        def has_slots(e):
            return pend_ref[e + 1] > pend_ref[e]

        def unused(j):
            start = pl.multiple_of(pend_ref[n_experts] + j * MOE_BLOCK, MOE_BLOCK)
            return pltpu.make_async_copy(zero_ref, xs_hbm.at[pl.ds(start, MOE_BLOCK), :], zsem)

        def is_unused(j):
            return pend_ref[n_experts] + j * MOE_BLOCK < xs_hbm.shape[0]
        for e in range(n_experts):
            pl.when(has_slots(e))(lambda e=e: tail(e).start())
        for j in range(n_experts):
            pl.when(is_unused(j))(lambda j=j: unused(j).start())
        for e in range(n_experts):
            pl.when(has_slots(e))(lambda e=e: tail(e).wait())
        for j in range(n_experts):
            pl.when(is_unused(j))(lambda j=j: unused(j).wait())

    base = i * tm * TOP_K

    def issue(t, carry):
        for k in range(TOP_K):
            pltpu.make_async_copy(h_ref.at[pl.ds(t, 1), :],
                                  xs_hbm.at[pl.ds(dest_ref[base + t * TOP_K + k], 1), :], sem).start(priority=k % 2)
        return carry
    lax.fori_loop(0, tm, issue, 0, unroll=2)
    for k in range(TOP_K):
        pltpu.make_async_copy(h_ref, xs_hbm.at[pl.ds(0, tm), :], sem).wait()


def _moe_dispatch(dest, pend0, h2, cap, tm=512):
    n, d = h2.shape
    n_experts = pend0.shape[0] - 1
    grid_spec = pltpu.PrefetchScalarGridSpec(
        num_scalar_prefetch=2, grid=(n // tm,),
        in_specs=[pl.BlockSpec((tm, d), lambda i, *_: (i, 0))],
        out_specs=pl.BlockSpec(memory_space=pl.ANY),
        scratch_shapes=[pltpu.VMEM((MOE_BLOCK, d), h2.dtype), pltpu.SemaphoreType.DMA, pltpu.SemaphoreType.DMA])
    return pl.pallas_call(
        functools.partial(_moe_dispatch_kernel, tm=tm, n_experts=n_experts), grid_spec=grid_spec,
        out_shape=jax.ShapeDtypeStruct((cap, d), h2.dtype),
        compiler_params=_cparams(("arbitrary",)),
        name="moe_dispatch",
    )(dest, pend0, h2)


def _moe_expert_kernel(be_ref, nu_ref, xs_ref, w1_ref, b1_ref, w2_ref, b2_ref, y_ref, w1b_ref, w2b_ref, *, d_exp):
    i = pl.program_id(0)
    live = i < nu_ref[0]
    changed = jnp.logical_or(i == 0, be_ref[i] != be_ref[jnp.maximum(i - 1, 0)])

    @pl.when(live & changed)
    def _():
        w1b_ref[...] = w1_ref[0].astype(BF16)
        w2b_ref[...] = w2_ref[0].astype(BF16)

    @pl.when(live)
    def _():
        hid = jnp.dot(xs_ref[...].astype(BF16), w1b_ref[...], preferred_element_type=F32) + b1_ref[0]
        glu = jnp.minimum(hid[:, :d_exp], SWIGLU_LIMIT)
        lin = jnp.clip(hid[:, d_exp:], -SWIGLU_LIMIT, SWIGLU_LIMIT)
        act = glu * jax.nn.sigmoid(SWIGLU_ALPHA * glu) * (lin + 1.0)
        y_ref[...] = jnp.dot(act.astype(BF16), w2b_ref[...], preferred_element_type=F32) + b2_ref[0]

    @pl.when(jnp.logical_not(live))
    def _():
        y_ref[...] = jnp.zeros(y_ref.shape, y_ref.dtype)


def _moe_experts(block_e, n_used, xs, w1, b1, w2, b2):
    cap, d = xs.shape
    d2 = w1.shape[-1]
    d_exp = d2 // 2
    n_exp = w1.shape[0] * w1.shape[1]
    w1 = w1.reshape(n_exp, d, d2)
    w2 = w2.reshape(n_exp, d_exp, d)
    grid_spec = pltpu.PrefetchScalarGridSpec(
        num_scalar_prefetch=2, grid=(cap // MOE_BLOCK,),
        in_specs=[pl.BlockSpec((MOE_BLOCK, d), lambda i, be, nu: (jnp.minimum(i, nu[0] - 1), 0)),
                  pl.BlockSpec((1, d, d2), lambda i, be, nu: (be[i], 0, 0)),
                  pl.BlockSpec((1, 1, d2), lambda i, be, nu: (be[i], 0, 0)),
                  pl.BlockSpec((1, d_exp, d), lambda i, be, nu: (be[i], 0, 0)),
                  pl.BlockSpec((1, 1, d), lambda i, be, nu: (be[i], 0, 0))],
        out_specs=pl.BlockSpec((MOE_BLOCK, d), lambda i, be, nu: (i, 0)),
        scratch_shapes=[pltpu.VMEM((d, d2), BF16), pltpu.VMEM((d_exp, d), BF16)])
    return pl.pallas_call(
        functools.partial(_moe_expert_kernel, d_exp=d_exp), grid_spec=grid_spec,
        out_shape=jax.ShapeDtypeStruct((cap, d), F32),
        compiler_params=_cparams(("arbitrary",)),
        name="moe_experts",
    )(block_e, n_used, xs, w1, b1.reshape(n_exp, 1, d2), w2, b2.reshape(n_exp, 1, d))


def _moe_combine_kernel(dest_ref, ys_hbm, x1_ref, gate_ref, gf_ref, gfin_ref, o_ref, rows_ref, sem, *, tm, final):
    base = pl.program_id(0) * tm * TOP_K

    def issue(t, carry):
        for k in range(TOP_K):
            _row_copy(ys_hbm, dest_ref[base + t * TOP_K + k], rows_ref.at[k], t, sem).start(priority=k % 2)
        return carry
    lax.fori_loop(0, tm, issue, 0, unroll=2)
    for k in range(TOP_K):
        pltpu.make_async_copy(ys_hbm.at[pl.ds(0, tm), :], rows_ref.at[k], sem).wait()
    moe = gate_ref[:, 0:1] * rows_ref[0]
    for k in range(1, TOP_K):
        moe = moe + gate_ref[:, k:k + 1] * rows_ref[k]
    out = x1_ref[...] + gf_ref[0] * moe
    if final:
        out = _rms(out, gfin_ref[...])
    o_ref[...] = out


def _moe_combine(dest, ys, x1, gate, gf, g_final, seq, final, tm=256):
    n, d = x1.shape
    tpb = seq // tm
    grid_spec = pltpu.PrefetchScalarGridSpec(
        num_scalar_prefetch=1, grid=(n // tm,),
        in_specs=[pl.BlockSpec(memory_space=pl.ANY),
                  pl.BlockSpec((tm, d), lambda i, *_: (i, 0)),
                  pl.BlockSpec((tm, LANES), lambda i, *_: (i, 0)),
                  pl.BlockSpec((1, 1, d), lambda i, *_: (i // tpb, 0, 0)),
                  pl.BlockSpec((1, d), lambda i, *_: (0, 0))],
        out_specs=pl.BlockSpec((tm, d), lambda i, *_: (i, 0)),
        scratch_shapes=[pltpu.VMEM((TOP_K, tm, d), F32), pltpu.SemaphoreType.DMA])
    return pl.pallas_call(
        functools.partial(_moe_combine_kernel, tm=tm, final=final), grid_spec=grid_spec,
        out_shape=jax.ShapeDtypeStruct((n, d), F32),
        compiler_params=_cparams(("arbitrary",)),
        name="moe_combine",
    )(dest, ys, x1, gate, gf, g_final)


def _route_tables(top_e, rank, counts, n_blocks):
    n_experts = counts.shape[0]
    padded = (counts + MOE_BLOCK - 1) // MOE_BLOCK * MOE_BLOCK
    pend = jnp.cumsum(padded)
    pstart = pend - padded
    onehot = top_e[..., None] == jnp.arange(n_experts, dtype=I32)
    dest = (jnp.sum(jnp.where(onehot, pstart, 0), axis=-1) + rank).reshape(-1).astype(I32)
    block_e = jnp.minimum(jnp.sum(pend[None, :] <= jnp.arange(n_blocks, dtype=I32)[:, None] * MOE_BLOCK, axis=1),
                          n_experts - 1).astype(I32)
    n_used = (pend[-1:] // MOE_BLOCK).astype(I32)
    pend0 = jnp.concatenate([jnp.zeros((1,), I32), pend.astype(I32)])
    return dest, pend0, block_e, n_used


def kernel(x, c, positions, rel_bias, norm_mix, w_ada, b_ada, w_in, q_norm, w_uq, kv_norm, w_ukv, w_out, norm_ffn, w_router, b_router, w1, b1, w2, b2, norm_final):
    bsz, seq, d = x.shape
    depth = w_ada.shape[0]
    n = bsz * seq
    n_experts = w_router.shape[-1]
    mod = _ada_mod(c, w_ada, b_ada)
    ctab, stab = _rope_tables(positions)
    row = lambda v: v.reshape(1, -1)
    x2 = x.reshape(n, d)
    for l in range(depth):
        sh_m, sc_m, g_m, sh_f, sc_f, g_f = [mod[l, :, i * d:(i + 1) * d].reshape(bsz, 1, d) for i in range(6)]
        wuq, wuqs = _pack_uq(w_uq[l])
        wukk, wukv = _pack_ukv(w_ukv[l])
        qa, ka, va, qi, ki, wi, qb, kb, vb = _proj(
            x2, row(norm_mix[l]), sc_m, sh_m, _pack_in_weights(w_in[l]), row(q_norm[l]), row(kv_norm[l]),
            wuq, wuqs, wukk, wukv, ctab, stab, seq)
        ya = _attn_a(qa, qi, wi, ka, va, ki, positions, rel_bias)
        yb = _attn_b(qb, kb, vb, positions)
        x1, h2, top_e, gate, rank, counts = _out_router(x2, ya, yb, w_out[l].astype(BF16), g_m, row(norm_ffn[l]),
                                                        sc_f, sh_f, w_router[l], b_router[l], seq)
        n_blocks = -(-n * TOP_K // MOE_BLOCK) + n_experts
        dest, pend0, block_e, n_used = _route_tables(top_e[:, :TOP_K], rank[:, :TOP_K], counts[0, :n_experts],
                                                     n_blocks)
        xs = _moe_dispatch(dest, pend0, h2, n_blocks * MOE_BLOCK)
        ys = _moe_experts(block_e + l * n_experts, n_used, xs, w1, b1, w2, b2)
        x2 = _moe_combine(dest, ys, x1, gate, g_f, row(norm_final), seq, final=(l == depth - 1))
    return x2.reshape(bsz, seq, d)
```

```python
import functools
import math

import numpy as np
import jax
import jax.numpy as jnp
from jax import lax
from jax.experimental import pallas as pl
from jax.experimental.pallas import tpu as pltpu

CHUNK = 64
EPS = 1e-6
A_HEADS = 8
A_HEAD_DIM = 64
IDX_HEADS = 8
IDX_DIM = 64
TOPK_MAX = 256
B_HEADS = 8
Q_LORA = 256
KV_LORA = 128
QK_NOPE = 64
QK_ROPE = 32
V_DIM = 64
ROPE_THETA = 10000.0
N_BUCKETS = 32
MAX_DISTANCE = 128
TOP_K = 4
SWIGLU_LIMIT = 7.0
SWIGLU_ALPHA = 1.702
MOE_BLOCK = 256

LANES = 128
SUBLANES = 8
VMEM_LIMIT = 56 * 1024 * 1024
INT_MIN = -2 ** 31
NEG_INF = float("-inf")
BIAS_ROWS = 32

F32 = jnp.float32
BF16 = jnp.bfloat16
I32 = jnp.int32

_T5_EXACT = (N_BUCKETS // 2) // 2
_T5_THRESH = tuple(
    int(math.ceil(_T5_EXACT * (MAX_DISTANCE / _T5_EXACT) ** (k / (N_BUCKETS // 2 - _T5_EXACT)) - 1e-9))
    for k in range(1, N_BUCKETS // 2 - _T5_EXACT))
_T5_FAR = _T5_THRESH[-1]


def _cparams(sem, vmem=VMEM_LIMIT, **kw):
    return pltpu.CompilerParams(dimension_semantics=sem, vmem_limit_bytes=vmem, **kw)


def _dot_t(a, b):
    return lax.dot_general(a, b, (((1,), (1,)), ((), ())), preferred_element_type=F32)


def _ada_kernel(c_ref, w_ref, b_ref, o_ref):
    c = c_ref[...]
    cond = c * jax.nn.sigmoid(c)
    o_ref[0] = jnp.dot(cond, w_ref[0], preferred_element_type=F32,
                       precision=lax.Precision.HIGHEST) + b_ref[0]


def _ada_mod(c, w_ada, b_ada, tn=512):
    depth, d, n6 = w_ada.shape
    b = c.shape[0]
    return pl.pallas_call(
        _ada_kernel,
        grid=(depth, n6 // tn),
        in_specs=[pl.BlockSpec((b, d), lambda l, j: (0, 0)),
                  pl.BlockSpec((1, d, tn), lambda l, j: (l, 0, j)),
                  pl.BlockSpec((1, 1, tn), lambda l, j: (l, 0, j))],
        out_specs=pl.BlockSpec((1, b, tn), lambda l, j: (l, 0, j)),
        out_shape=jax.ShapeDtypeStruct((depth, b, n6), F32),
        compiler_params=_cparams(("arbitrary", "arbitrary")),
        name="ada_mod",
    )(c, w_ada, b_ada.reshape(depth, 1, n6))


D_A = A_HEADS * A_HEAD_DIM
D_B = B_HEADS * V_DIM
HB = 128
_C_QA, _C_KA, _C_VA, _C_QI = 0, D_A, 2 * D_A, 3 * D_A
_C_KI = 4 * D_A
_C_WI = _C_KI + LANES
_C_CQ = _C_WI + LANES
_C_CKV = _C_CQ + Q_LORA
_C_KR = _C_CKV + KV_LORA
_C_KRS = _C_KR + LANES
_C_END = _C_KRS + LANES


def _pack_in_weights(w_in):
    d = w_in.shape[0]
    offs = np.cumsum((0, D_A, D_A, D_A, IDX_HEADS * IDX_DIM, IDX_DIM, IDX_HEADS, Q_LORA, KV_LORA, QK_ROPE))
    seg = [w_in[:, offs[i]:offs[i + 1]] for i in range(9)]
    q_a, k_a, v_a, q_i, k_i, w_i, c_q, c_kv, k_r = seg
    z = lambda n: jnp.zeros((d, n), w_in.dtype)
    half = QK_ROPE // 2
    k_rs = jnp.concatenate([k_r[:, half:], k_r[:, :half]], axis=1)
    cols = [q_a * (A_HEAD_DIM ** -0.5), k_a, v_a, q_i, k_i, k_i, w_i, z(LANES - IDX_HEADS), c_q, c_kv,
            z(QK_NOPE), k_r, z(HB - QK_NOPE - QK_ROPE), z(QK_NOPE), k_rs, z(HB - QK_NOPE - QK_ROPE)]
    return jnp.concatenate(cols, axis=1).astype(BF16)


def _pack_uq(w_uq):
    r = w_uq.shape[0]
    w = w_uq.reshape(r, B_HEADS, QK_NOPE + QK_ROPE)
    nope, rope = w[..., :QK_NOPE], w[..., QK_NOPE:]
    half = QK_ROPE // 2
    z = jnp.zeros((r, B_HEADS, HB - QK_NOPE - QK_ROPE), w.dtype)
    main = jnp.concatenate([nope, rope, z], axis=-1).reshape(r, B_HEADS * HB)
    swap = jnp.concatenate([jnp.zeros_like(nope), rope[..., half:], rope[..., :half], z], axis=-1)
    return main.astype(BF16), swap.reshape(r, B_HEADS * HB).astype(BF16)


def _pack_ukv(w_ukv):
    r = w_ukv.shape[0]
    w = w_ukv.reshape(r, B_HEADS, QK_NOPE + V_DIM)
    k = jnp.concatenate([w[..., :QK_NOPE], jnp.zeros((r, B_HEADS, HB - QK_NOPE), w.dtype)], axis=-1)
    v = w[..., QK_NOPE:]
    return k.reshape(r, B_HEADS * HB).astype(BF16), v.reshape(r, B_HEADS * V_DIM).astype(BF16)


def _rope_tables(positions):
    half = QK_ROPE // 2
    inv = ROPE_THETA ** (-jnp.arange(half, dtype=F32) / half)
    ang = positions.astype(F32).reshape(-1, 1) * inv[None, :]
    cos, sin = jnp.cos(ang), jnp.sin(ang)
    n = ang.shape[0]
    one = jnp.ones((n, QK_NOPE), F32)
    z = lambda k: jnp.zeros((n, k), F32)
    ctab = jnp.concatenate([one, cos, cos, z(HB - QK_NOPE - QK_ROPE)], axis=1)
    stab = jnp.concatenate([z(QK_NOPE), -sin, sin, z(HB - QK_NOPE - QK_ROPE)], axis=1)
    return ctab, stab


def _rms(x, g):
    return x * lax.rsqrt(jnp.mean(x * x, axis=-1, keepdims=True) + EPS) * g


def _proj_kernel(x_ref, g_ref, sc_ref, sh_ref, wp_ref, qn_ref, kvn_ref, wuq_ref, wuqs_ref, wukk_ref, wukv_ref,
                 ct_ref, st_ref,
                 qa_ref, ka_ref, va_ref, qi_ref, ki_ref, wi_ref, qb_ref, kb_ref, vb_ref):
    x = x_ref[...]
    h = _rms(x, g_ref[...]) * (1.0 + sc_ref[0]) + sh_ref[0]
    hb = h.astype(BF16)
    seg = lambda a, b: jnp.dot(hb, wp_ref[:, a:b], preferred_element_type=F32)
    qa_ref[...] = seg(_C_QA, _C_KA).astype(BF16)
    ka_ref[...] = seg(_C_KA, _C_VA).astype(BF16)
    va_ref[...] = seg(_C_VA, _C_QI).astype(BF16)
    qi_ref[...] = seg(_C_QI, _C_KI).astype(BF16)
    ki_ref[...] = seg(_C_KI, _C_WI).astype(BF16)
    wi_ref[...] = seg(_C_WI, _C_CQ)
    ct = ct_ref[...]
    st = st_ref[...]
    cq = _rms(seg(_C_CQ, _C_CKV), qn_ref[...]).astype(BF16)
    scale = (QK_NOPE + QK_ROPE) ** -0.5
    for hd in range(B_HEADS):
        sl = slice(hd * HB, (hd + 1) * HB)
        q = jnp.dot(cq, wuq_ref[:, sl], preferred_element_type=F32)
        qs = jnp.dot(cq, wuqs_ref[:, sl], preferred_element_type=F32)
        qb_ref[:, sl] = ((q * ct + qs * st) * scale).astype(BF16)
    ckv = _rms(seg(_C_CKV, _C_KR), kvn_ref[...]).astype(BF16)
    kr = seg(_C_KR, _C_KRS) * ct + seg(_C_KRS, _C_END) * st
    for hd in range(B_HEADS):
        sl = slice(hd * HB, (hd + 1) * HB)
        kb_ref[:, sl] = (jnp.dot(ckv, wukk_ref[:, sl], preferred_element_type=F32) + kr).astype(BF16)
    vb_ref[...] = jnp.dot(ckv, wukv_ref[...], preferred_element_type=F32).astype(BF16)


def _proj(x2, g, sc, sh, wp, qn, kvn, wuq, wuqs, wukk, wukv, ctab, stab, seq, tm=256):
    n, d = x2.shape
    tpb = seq // tm
    row = lambda w: pl.BlockSpec((tm, w), lambda i: (i, 0))
    full = lambda a: pl.BlockSpec(a.shape, lambda i: (0,) * a.ndim)
    per_b = pl.BlockSpec((1, 1, d), lambda i: (i // tpb, 0, 0))
    outs = [(D_A, BF16), (D_A, BF16), (D_A, BF16), (D_A, BF16), (LANES, BF16), (LANES, F32),
            (B_HEADS * HB, BF16), (B_HEADS * HB, BF16), (D_B, BF16)]
    return pl.pallas_call(
        _proj_kernel,
        grid=(n // tm,),
        in_specs=[row(d), full(g), per_b, per_b, full(wp), full(qn), full(kvn), full(wuq), full(wuqs),
                  full(wukk), full(wukv), row(HB), row(HB)],
        out_specs=[row(w) for w, _ in outs],
        out_shape=[jax.ShapeDtypeStruct((n, w), dt) for w, dt in outs],
        compiler_params=_cparams(("arbitrary",)),
        name="in_proj",
    )(x2, g, sc, sh, wp, qn, kvn, wuq, wuqs, wukk, wukv, ctab, stab)


def _pair_mask(x_pair, odd):
    lane = lax.broadcasted_iota(I32, x_pair.shape, 1)
    keep = (lane >= A_HEAD_DIM) if odd else (lane < A_HEAD_DIM)
    return jnp.where(keep, x_pair, jnp.zeros_like(x_pair))


def _flash_step(hd, s, v_pair, m_ref, l_ref, acc_ref):
    m_prev = m_ref[hd]
    m_new = jnp.maximum(m_prev, jnp.max(s, axis=1, keepdims=True))
    m_safe = jnp.where(m_new == NEG_INF, 0.0, m_new)
    alpha = jnp.exp(m_prev - m_safe)
    p = jnp.exp(s - jnp.concatenate([m_safe] * (s.shape[1] // LANES), axis=1))
    l_ref[hd] = alpha * l_ref[hd] + jnp.sum(p, axis=1, keepdims=True)
    acc_ref[hd] = alpha * acc_ref[hd] + jnp.dot(p.astype(BF16), v_pair, preferred_element_type=F32)
    m_ref[hd] = m_new


def _flash_init(m_ref, l_ref, acc_ref):
    m_ref[...] = jnp.full(m_ref.shape, NEG_INF, F32)
    l_ref[...] = jnp.zeros(l_ref.shape, F32)
    acc_ref[...] = jnp.zeros(acc_ref.shape, F32)


def _flash_finish(y_ref, l_ref, acc_ref, n_heads):
    lane = lax.broadcasted_iota(I32, acc_ref.shape[1:], 1)
    for j in range(n_heads // 2):
        even = acc_ref[2 * j] / l_ref[2 * j]
        odd = acc_ref[2 * j + 1] / l_ref[2 * j + 1]
        y_ref[:, j * LANES:(j + 1) * LANES] = jnp.where(lane < V_DIM, even, odd).astype(y_ref.dtype)


def _t5_bias_heads(rel, bias_ref, lo, hi):
    nb = N_BUCKETS // 2
    n = jnp.abs(rel)
    bucket = jnp.minimum(n, _T5_EXACT)
    for t in _T5_THRESH:
        bucket = bucket + (n >= t).astype(I32)
    if hi >= nb:
        bucket = bucket + (nb if lo >= nb else jnp.where(rel > 0, nb, 0))
    outs = [jnp.full(rel.shape, bias_ref[lo * A_HEADS + hd], F32) for hd in range(A_HEADS)]
    for j in range(lo + 1, hi + 1):
        hit = bucket == j
        outs = [jnp.where(hit, bias_ref[j * A_HEADS + hd], outs[hd]) for hd in range(A_HEADS)]
    return outs


def _attn_a_kernel(nk_ref, qlo_ref, qhi_ref, klo_ref, khi_ref, rlo_ref, rhi_ref,
                   bias_ref, qa_ref, qi_ref, wi_ref, ka_ref, va_ref, ki_ref, pq_ref, pk_ref,
                   y_ref,
                   key_ref, keyt_ref, am_ref, ex_ref, j_ref, wb_ref, m_ref, l_ref, acc_ref,
                   *, topk, tq, tk, nq, nkt):
    b = pl.program_id(0)
    qi_idx = pl.program_id(1)
    nk = nk_ref[b * nq + qi_idx]
    pq = pq_ref[...]
    qchunk = jnp.right_shift(pq, int(math.log2(CHUNK)))
    lane_tk = lax.broadcasted_iota(I32, (tq, tk), 1)

    wide = lambda v: jnp.concatenate([v] * (tk // LANES), axis=1)
    for hd in range(IDX_HEADS):
        wb_ref[hd] = jnp.broadcast_to(wi_ref[:, hd:hd + 1], (tq, LANES))

    def score_body(kt, carry):
        ks = ki_ref[pl.ds(pl.multiple_of(kt * tk, tk), tk), :]
        sc = jnp.zeros((tq, tk), F32)
        for hd in range(IDX_HEADS):
            pair = qi_ref[:, (hd // 2) * LANES:(hd // 2 + 1) * LANES]
            r = _dot_t(_pair_mask(pair, hd % 2), ks)
            sc = sc + wide(wb_ref[hd]) * jnp.maximum(r, 0.0)
        bits = pltpu.bitcast(sc, I32)
        key = bits ^ (jnp.right_shift(bits, 31) & jnp.int32(0x7FFFFFFF))
        kchunk = jnp.right_shift(pk_ref[kt], int(math.log2(CHUNK)))
        key = jnp.where(kchunk <= qchunk, key, jnp.int32(INT_MIN))
        key_ref[kt] = key
        keyt_ref[kt] = key.T
        return carry
    lax.fori_loop(0, nk, score_body, 0)

    groups = tk // SUBLANES
    sub_idx = (lax.broadcasted_iota(I32, (groups, SUBLANES, tq), 0) * SUBLANES
               + lax.broadcasted_iota(I32, (groups, SUBLANES, tq), 1))

    def count(pred):
        def body(kt, acc):
            k3 = keyt_ref[kt].reshape(groups, SUBLANES, tq)
            return acc + jnp.sum(jnp.where(pred(k3, kt * tk), 1, 0), axis=0)
        acc = lax.fori_loop(0, nk, body, jnp.zeros((SUBLANES, tq), I32))
        return jnp.broadcast_to(jnp.sum(acc, axis=0, keepdims=True), (SUBLANES, tq))

    def bit_body(i, t):
        cand = t + jnp.left_shift(jnp.int32(1), 31 - i)
        c = count(lambda k, k0: k >= cand)
        return jnp.where(c >= topk, cand, t)
    thr_t = lax.fori_loop(0, 32, bit_body, jnp.full((SUBLANES, tq), INT_MIN, I32))
    need = topk - count(lambda k, k0: k > thr_t)
    n_eq = count(lambda k, k0: k == thr_t)
    real = thr_t > jnp.int32(INT_MIN)
    j_ref[...] = jnp.where(real, jnp.int32(nkt * tk), jnp.int32(-1))
    tie = jnp.max(jnp.where(real & (n_eq > need), 1, 0))

    @pl.when(tie > 0)
    def _():
        def idx_body(i, p):
            cand = p + jnp.left_shift(jnp.int32(1), (nkt * tk).bit_length() - 2 - i)
            c = count(lambda k, k0: (k == thr_t) & (sub_idx + k0 < cand))
            return jnp.where(c < need, cand, p)
        p = lax.fori_loop(0, (nkt * tk).bit_length() - 1, idx_body, jnp.zeros((SUBLANES, tq), I32))
        j_ref[...] = jnp.where(real & (n_eq > need), p, j_ref[...])

    to_rows = lambda v: jnp.broadcast_to(v[:1], (LANES, tq)).T
    thr_w = wide(to_rows(thr_t))
    jsel_w = wide(to_rows(j_ref[...]))

    def mask_body(kt, carry):
        k = key_ref[kt]
        sel = (k > thr_w) | ((k == thr_w) & (lane_tk + kt * tk <= jsel_w))
        am_ref[kt] = jnp.where(sel, 0.0, NEG_INF)
        return carry
    lax.fori_loop(0, nk, mask_body, 0)

    _flash_init(m_ref, l_ref, acc_ref)
    q_lo = qlo_ref[b * nq + qi_idx]
    q_hi = qhi_ref[b * nq + qi_idx]

    shift = int(math.log2(CHUNK))
    chunk0 = b * (nkt * tk // LANES)
    block0 = b * (nq * tq // BIAS_ROWS) + qi_idx * (tq // BIAS_ROWS)

    def bias_chunk(kt, c):
        cs = slice(c * LANES, (c + 1) * LANES)
        k_lo = klo_ref[chunk0 + kt * (tk // LANES) + c]
        k_hi = khi_ref[chunk0 + kt * (tk // LANES) + c]
        far_past = k_hi - q_lo <= -_T5_FAR
        far_future = k_lo - q_hi >= _T5_FAR

        @pl.when(far_past)
        def _():
            for hd in range(A_HEADS):
                ex_ref[hd, :, cs] = am_ref[kt, :, cs]

        @pl.when(far_future)
        def _():
            for hd in range(A_HEADS):
                ex_ref[hd, :, cs] = am_ref[kt, :, cs] + bias_ref[(N_BUCKETS - 1) * A_HEADS + hd]

        @pl.when(jnp.logical_not(far_past | far_future))
        def _():
            pkc = pk_ref[kt][:, cs]

            def rows_body(r, carry):
                rows = pl.ds(pl.multiple_of(r * BIAS_ROWS, BIAS_ROWS), BIAS_ROWS)
                r_lo = rlo_ref[block0 + r]
                r_hi = rhi_ref[block0 + r]
                hidden = jnp.right_shift(k_lo, shift) > jnp.right_shift(r_hi, shift)
                behind = k_hi <= r_lo
                ahead = k_lo > r_hi

                def emit(lo, hi):
                    am = am_ref[kt, rows, cs]
                    for hd, bias in enumerate(_t5_bias_heads(pkc - pq_ref[rows, :], bias_ref, lo, hi)):
                        ex_ref[hd, rows, cs] = am + bias

                @pl.when(hidden)
                def _():
                    for hd in range(A_HEADS):
                        ex_ref[hd, rows, cs] = am_ref[kt, rows, cs]
                pl.when(behind)(lambda: emit(0, N_BUCKETS // 2 - 1))
                pl.when(ahead & jnp.logical_not(hidden))(lambda: emit(N_BUCKETS // 2, N_BUCKETS - 1))
                pl.when(jnp.logical_not(behind | ahead))(lambda: emit(0, N_BUCKETS - 1))
                return carry
            lax.fori_loop(0, tq // BIAS_ROWS, rows_body, 0)

    def flash_heads(kt, extra):
        start = pl.multiple_of(kt * tk, tk)
        for hd in range(A_HEADS):
            pr = slice((hd // 2) * LANES, (hd // 2 + 1) * LANES)
            qm = _pair_mask(qa_ref[:, pr], hd % 2)
            s = _dot_t(qm, ka_ref[pl.ds(start, tk), pr]) + extra(hd)
            _flash_step(hd, s, va_ref[pl.ds(start, tk), pr], m_ref, l_ref, acc_ref)

    def attn_body(kt, carry):
        tile_hi = khi_ref[chunk0 + kt * (tk // LANES)]
        for c in range(1, tk // LANES):
            tile_hi = jnp.maximum(tile_hi, khi_ref[chunk0 + kt * (tk // LANES) + c])
        far_tile = tile_hi - q_lo <= -_T5_FAR

        @pl.when(far_tile)
        def _():
            flash_heads(kt, lambda hd: am_ref[kt])

        @pl.when(jnp.logical_not(far_tile))
        def _():
            for c in range(tk // LANES):
                bias_chunk(kt, c)
            flash_heads(kt, lambda hd: ex_ref[hd])
        return carry
    lax.fori_loop(0, nk, attn_body, 0)
    _flash_finish(y_ref, l_ref, acc_ref, A_HEADS)


def _tile_tables(positions, tq, tk):
    b, s = positions.shape
    pq = positions.reshape(b, s // tq, tq)
    pk = positions.reshape(b, s // tk, tk)
    pc = positions.reshape(b, s // LANES, LANES)
    pr = positions.reshape(b, s // BIAS_ROWS, BIAS_ROWS)
    q_lo, q_hi = pq.min(-1), pq.max(-1)
    vis = (pk.min(-1) // CHUNK)[:, None, :] <= (q_hi // CHUNK)[:, :, None]
    last = jnp.max(jnp.where(vis, jnp.arange(s // tk, dtype=I32)[None, None, :] + 1, 0), axis=-1)
    flat = lambda a: a.reshape(-1).astype(I32)
    return (flat(last), flat(q_lo), flat(q_hi), flat(pc.min(-1)), flat(pc.max(-1)),
            flat(pr.min(-1)), flat(pr.max(-1)))


def _attn_a(qa, qi, wi, ka, va, ki, positions, rel_bias, tq=256, tk=512):
    b, s = positions.shape
    n = b * s
    nq, nkt = s // tq, s // tk
    topk = min(TOPK_MAX, s // 4)
    tables = _tile_tables(positions, tq, tk)
    pos_col = positions.reshape(n, 1)
    pos_row = positions.reshape(b, nkt, 1, tk)
    qrow = lambda w: pl.BlockSpec((tq, w), lambda bi, i, *_: (bi * nq + i, 0))
    kfull = lambda w: pl.BlockSpec((s, w), lambda bi, i, *_: (bi, 0))
    kern = functools.partial(_attn_a_kernel, topk=topk, tq=tq, tk=tk, nq=nq, nkt=nkt)
    grid_spec = pltpu.PrefetchScalarGridSpec(
        num_scalar_prefetch=len(tables),
        grid=(b, nq),
        in_specs=[pl.BlockSpec(memory_space=pltpu.SMEM),
                  qrow(D_A), qrow(D_A), qrow(LANES), kfull(D_A), kfull(D_A), kfull(LANES),
                  pl.BlockSpec((tq, 1), lambda bi, i, *_: (bi * nq + i, 0)),
                  pl.BlockSpec((None, nkt, 1, tk), lambda bi, i, *_: (bi, 0, 0, 0))],
        out_specs=qrow(D_A),
        scratch_shapes=[pltpu.VMEM((nkt, tq, tk), I32), pltpu.VMEM((nkt, tk, tq), I32),
                        pltpu.VMEM((nkt, tq, tk), F32),
                        pltpu.VMEM((A_HEADS, tq, tk), F32), pltpu.VMEM((SUBLANES, tq), I32),
                        pltpu.VMEM((IDX_HEADS, tq, LANES), F32),
                        pltpu.VMEM((A_HEADS, tq, LANES), F32), pltpu.VMEM((A_HEADS, tq, LANES), F32),
                        pltpu.VMEM((A_HEADS, tq, LANES), F32)])
    return pl.pallas_call(
        kern, grid_spec=grid_spec,
        out_shape=jax.ShapeDtypeStruct((n, D_A), BF16),
        compiler_params=_cparams(("arbitrary", "arbitrary")),
        name="attn_indexer",
    )(*tables, (rel_bias - rel_bias[N_BUCKETS // 2 - 1]).reshape(-1), qa, qi, wi, ka, va, ki, pos_col, pos_row)


def _attn_b_kernel(nk_ref, qb_ref, kb_ref, vb_ref, pq_ref, pk_ref, y_ref, m_ref, l_ref, acc_ref, *, tq, tk, nq):
    b = pl.program_id(0)
    nk = nk_ref[b * nq + pl.program_id(1)]
    qchunk = jnp.right_shift(pq_ref[...], int(math.log2(CHUNK)))
    _flash_init(m_ref, l_ref, acc_ref)

    def body(kt, carry):
        start = pl.multiple_of(kt * tk, tk)
        kchunk = jnp.right_shift(pk_ref[kt], int(math.log2(CHUNK)))
        am = jnp.where(kchunk <= qchunk, 0.0, NEG_INF)
        for hd in range(B_HEADS):
            hs = slice(hd * HB, (hd + 1) * HB)
            pr = slice((hd // 2) * LANES, (hd // 2 + 1) * LANES)
            s = _dot_t(qb_ref[:, hs], kb_ref[pl.ds(start, tk), hs]) + am
            _flash_step(hd, s, vb_ref[pl.ds(start, tk), pr], m_ref, l_ref, acc_ref)
        return carry
    lax.fori_loop(0, nk, body, 0)
    _flash_finish(y_ref, l_ref, acc_ref, B_HEADS)


def _attn_b(qb, kb, vb, positions, tq=256, tk=512):
    b, s = positions.shape
    n = b * s
    nq, nkt = s // tq, s // tk
    nk = _tile_tables(positions, tq, tk)[0]
    qrow = lambda w: pl.BlockSpec((tq, w), lambda bi, i, *_: (bi * nq + i, 0))
    kfull = lambda w: pl.BlockSpec((s, w), lambda bi, i, *_: (bi, 0))
    grid_spec = pltpu.PrefetchScalarGridSpec(
        num_scalar_prefetch=1,
        grid=(b, nq),
        in_specs=[qrow(B_HEADS * HB), kfull(B_HEADS * HB), kfull(D_B),
                  pl.BlockSpec((tq, 1), lambda bi, i, *_: (bi * nq + i, 0)),
                  pl.BlockSpec((None, nkt, 1, tk), lambda bi, i, *_: (bi, 0, 0, 0))],
        out_specs=qrow(D_B),
        scratch_shapes=[pltpu.VMEM((B_HEADS, tq, LANES), F32)] * 3)
    return pl.pallas_call(
        functools.partial(_attn_b_kernel, tq=tq, tk=tk, nq=nq), grid_spec=grid_spec,
        out_shape=jax.ShapeDtypeStruct((n, D_B), BF16),
        compiler_params=_cparams(("arbitrary", "arbitrary")),
        name="attn_latent",
    )(nk, qb, kb, vb, positions.reshape(n, 1), positions.reshape(b, nkt, 1, tk))


def _out_router_kernel(x_ref, ya_ref, yb_ref, wo_ref, gm_ref, g_ref, sc_ref, sh_ref, wr_ref, br_ref,
                       x1_ref, h2_ref, e_ref, gate_ref, rank_ref, cnt_ref, run_ref, *, n_experts):
    @pl.when(pl.program_id(0) == 0)
    def _():
        run_ref[...] = jnp.zeros(run_ref.shape, F32)

    mix = (jnp.dot(ya_ref[...], wo_ref[:D_A, :], preferred_element_type=F32)
           + jnp.dot(yb_ref[...], wo_ref[D_A:, :], preferred_element_type=F32))
    x1 = x_ref[...] + gm_ref[0] * mix
    x1_ref[...] = x1
    h2 = _rms(x1, g_ref[...]) * (1.0 + sc_ref[0]) + sh_ref[0]
    h2_ref[...] = h2
    logits = jnp.dot(h2, wr_ref[...], preferred_element_type=F32, precision=lax.Precision.HIGHEST) + br_ref[...]
    lane = lax.broadcasted_iota(I32, logits.shape, 1)
    cur = jnp.where(lane < n_experts, logits, NEG_INF)
    e_out = jnp.zeros(logits.shape, I32)
    g_out = jnp.zeros(logits.shape, F32)
    top = None
    picks = []
    for k in range(TOP_K):
        m = jnp.max(cur, axis=1, keepdims=True)
        idx = jnp.min(jnp.where(cur == m, lane, LANES), axis=1, keepdims=True)
        top = m if top is None else top
        e_out = jnp.where(lane == k, idx, e_out)
        g_out = jnp.where(lane == k, jnp.exp(m - top), g_out)
        picks.append(lane == idx)
        cur = jnp.where(picks[-1], NEG_INF, cur)
    e_ref[...] = e_out
    gate_ref[...] = g_out / jnp.sum(g_out, axis=1, keepdims=True)
    tm = logits.shape[0]
    chosen = jnp.where(picks[0] | picks[1] | picks[2] | picks[3], 1.0, 0.0)
    earlier = (lax.broadcasted_iota(I32, (tm, tm), 0) > lax.broadcasted_iota(I32, (tm, tm), 1))
    before = jnp.dot(jnp.where(earlier, 1.0, 0.0).astype(BF16), chosen.astype(BF16),
                     preferred_element_type=F32) + run_ref[0:1, :]
    r_out = jnp.zeros(logits.shape, I32)
    for k in range(TOP_K):
        rk = jnp.sum(jnp.where(picks[k], before, 0.0), axis=1, keepdims=True)
        r_out = jnp.where(lane == k, rk.astype(I32), r_out)
    rank_ref[...] = r_out
    run_ref[...] = run_ref[...] + jnp.sum(chosen, axis=0, keepdims=True)
    cnt_ref[...] = run_ref[...].astype(I32)


def _out_router(x2, ya, yb, wo, gm, g, sc, sh, wr, br, seq, tm=512):
    n, d = x2.shape
    tpb = seq // tm
    n_experts = wr.shape[1]
    wr_p = jnp.pad(wr, ((0, 0), (0, LANES - n_experts)))
    br_p = jnp.pad(br, (0, LANES - n_experts)).reshape(1, LANES)
    row = lambda w: pl.BlockSpec((tm, w), lambda i: (i, 0))
    full = lambda a: pl.BlockSpec(a.shape, lambda i: (0,) * a.ndim)
    per_b = pl.BlockSpec((1, 1, d), lambda i: (i // tpb, 0, 0))
    return pl.pallas_call(
        functools.partial(_out_router_kernel, n_experts=n_experts),
        grid=(n // tm,),
        in_specs=[row(d), row(D_A), row(D_B), full(wo), per_b, full(g), per_b, per_b, full(wr_p), full(br_p)],
        out_specs=[row(d), row(d), row(LANES), row(LANES), row(LANES),
                   pl.BlockSpec((SUBLANES, LANES), lambda i: (0, 0))],
        out_shape=[jax.ShapeDtypeStruct((n, d), F32), jax.ShapeDtypeStruct((n, d), F32),
                   jax.ShapeDtypeStruct((n, LANES), I32), jax.ShapeDtypeStruct((n, LANES), F32),
                   jax.ShapeDtypeStruct((n, LANES), I32), jax.ShapeDtypeStruct((SUBLANES, LANES), I32)],
        scratch_shapes=[pltpu.VMEM((SUBLANES, LANES), F32)],
        compiler_params=_cparams(("arbitrary",)),
        name="out_proj_router",
    )(x2, ya, yb, wo, gm, g, sc, sh, wr_p, br_p)


def _row_copy(src_hbm, row, dst_ref, slot, sem):
    return pltpu.make_async_copy(src_hbm.at[pl.ds(row, 1), :], dst_ref.at[pl.ds(slot, 1), :], sem)


def _moe_dispatch_kernel(dest_ref, pend_ref, h_ref, xs_hbm, zero_ref, zsem, sem, *, tm, n_experts):
    i = pl.program_id(0)

    @pl.when(i == 0)
    def _():
        zero_ref[...] = jnp.zeros(zero_ref.shape, zero_ref.dtype)

        def tail(e):
            start = pl.multiple_of(pend_ref[e + 1] - MOE_BLOCK, MOE_BLOCK)
            return pltpu.make_async_copy(zero_ref, xs_hbm.at[pl.ds(start, MOE_BLOCK), :], zsem)

        def has_slots(e):
            return pend_ref[e + 1] > pend_ref[e]

        def unused(j):
            start = pl.multiple_of(pend_ref[n_experts] + j * MOE_BLOCK, MOE_BLOCK)
            return pltpu.make_async_copy(zero_ref, xs_hbm.at[pl.ds(start, MOE_BLOCK), :], zsem)

        def is_unused(j):
            return pend_ref[n_experts] + j * MOE_BLOCK < xs_hbm.shape[0]
        for e in range(n_experts):
            pl.when(has_slots(e))(lambda e=e: tail(e).start())
        for j in range(n_experts):
            pl.when(is_unused(j))(lambda j=j: unused(j).start())
        for e in range(n_experts):
            pl.when(has_slots(e))(lambda e=e: tail(e).wait())
        for j in range(n_experts):
            pl.when(is_unused(j))(lambda j=j: unused(j).wait())

    base = i * tm * TOP_K

    def issue(t, carry):
        for k in range(TOP_K):
            pltpu.make_async_copy(h_ref.at[pl.ds(t, 1), :],
                                  xs_hbm.at[pl.ds(dest_ref[base + t * TOP_K + k], 1), :], sem).start(priority=k % 2)
        return carry
    lax.fori_loop(0, tm, issue, 0, unroll=2)
    for k in range(TOP_K):
        pltpu.make_async_copy(h_ref, xs_hbm.at[pl.ds(0, tm), :], sem).wait()


def _moe_dispatch(dest, pend0, h2, cap, tm=512):
    n, d = h2.shape
    n_experts = pend0.shape[0] - 1
    grid_spec = pltpu.PrefetchScalarGridSpec(
        num_scalar_prefetch=2, grid=(n // tm,),
        in_specs=[pl.BlockSpec((tm, d), lambda i, *_: (i, 0))],
        out_specs=pl.BlockSpec(memory_space=pl.ANY),
        scratch_shapes=[pltpu.VMEM((MOE_BLOCK, d), h2.dtype), pltpu.SemaphoreType.DMA, pltpu.SemaphoreType.DMA])
    return pl.pallas_call(
        functools.partial(_moe_dispatch_kernel, tm=tm, n_experts=n_experts), grid_spec=grid_spec,
        out_shape=jax.ShapeDtypeStruct((cap, d), h2.dtype),
        compiler_params=_cparams(("arbitrary",)),
        name="moe_dispatch",
    )(dest, pend0, h2)


def _moe_expert_kernel(be_ref, nu_ref, xs_ref, w1_ref, b1_ref, w2_ref, b2_ref, y_ref, w1b_ref, w2b_ref, *, d_exp):
    i = pl.program_id(0)
    live = i < nu_ref[0]
    changed = jnp.logical_or(i == 0, be_ref[i] != be_ref[jnp.maximum(i - 1, 0)])

    @pl.when(live & changed)
    def _():
        w1b_ref[...] = w1_ref[0].astype(BF16)
        w2b_ref[...] = w2_ref[0].astype(BF16)

    @pl.when(live)
    def _():
        hid = jnp.dot(xs_ref[...].astype(BF16), w1b_ref[...], preferred_element_type=F32) + b1_ref[0]
        glu = jnp.minimum(hid[:, :d_exp], SWIGLU_LIMIT)
        lin = jnp.clip(hid[:, d_exp:], -SWIGLU_LIMIT, SWIGLU_LIMIT)
        act = glu * jax.nn.sigmoid(SWIGLU_ALPHA * glu) * (lin + 1.0)
        y_ref[...] = jnp.dot(act.astype(BF16), w2b_ref[...], preferred_element_type=F32) + b2_ref[0]

    @pl.when(jnp.logical_not(live))
    def _():
        y_ref[...] = jnp.zeros(y_ref.shape, y_ref.dtype)


def _moe_experts(block_e, n_used, xs, w1, b1, w2, b2):
    cap, d = xs.shape
    d2 = w1.shape[-1]
    d_exp = d2 // 2
    n_exp = w1.shape[0] * w1.shape[1]
    w1 = w1.reshape(n_exp, d, d2)
    w2 = w2.reshape(n_exp, d_exp, d)
    grid_spec = pltpu.PrefetchScalarGridSpec(
        num_scalar_prefetch=2, grid=(cap // MOE_BLOCK,),
        in_specs=[pl.BlockSpec((MOE_BLOCK, d), lambda i, be, nu: (jnp.minimum(i, nu[0] - 1), 0)),
                  pl.BlockSpec((1, d, d2), lambda i, be, nu: (be[i], 0, 0)),
                  pl.BlockSpec((1, 1, d2), lambda i, be, nu: (be[i], 0, 0)),
                  pl.BlockSpec((1, d_exp, d), lambda i, be, nu: (be[i], 0, 0)),
                  pl.BlockSpec((1, 1, d), lambda i, be, nu: (be[i], 0, 0))],
        out_specs=pl.BlockSpec((MOE_BLOCK, d), lambda i, be, nu: (i, 0)),
        scratch_shapes=[pltpu.VMEM((d, d2), BF16), pltpu.VMEM((d_exp, d), BF16)])
    return pl.pallas_call(
        functools.partial(_moe_expert_kernel, d_exp=d_exp), grid_spec=grid_spec,
        out_shape=jax.ShapeDtypeStruct((cap, d), F32),
        compiler_params=_cparams(("arbitrary",)),
        name="moe_experts",
    )(block_e, n_used, xs, w1, b1.reshape(n_exp, 1, d2), w2, b2.reshape(n_exp, 1, d))


def _moe_combine_kernel(dest_ref, ys_hbm, x1_ref, gate_ref, gf_ref, gfin_ref, o_ref, rows_ref, sem, *, tm, final):
    base = pl.program_id(0) * tm * TOP_K

    def issue(t, carry):
        for k in range(TOP_K):
            _row_copy(ys_hbm, dest_ref[base + t * TOP_K + k], rows_ref.at[k], t, sem).start(priority=k % 2)
        return carry
    lax.fori_loop(0, tm, issue, 0, unroll=2)
    for k in range(TOP_K):
        pltpu.make_async_copy(ys_hbm.at[pl.ds(0, tm), :], rows_ref.at[k], sem).wait()
    moe = gate_ref[:, 0:1] * rows_ref[0]
    for k in range(1, TOP_K):
        moe = moe + gate_ref[:, k:k + 1] * rows_ref[k]
    out = x1_ref[...] + gf_ref[0] * moe
    if final:
        out = _rms(out, gfin_ref[...])
    o_ref[...] = out


def _moe_combine(dest, ys, x1, gate, gf, g_final, seq, final, tm=256):
    n, d = x1.shape
    tpb = seq // tm
    grid_spec = pltpu.PrefetchScalarGridSpec(
        num_scalar_prefetch=1, grid=(n // tm,),
        in_specs=[pl.BlockSpec(memory_space=pl.ANY),
                  pl.BlockSpec((tm, d), lambda i, *_: (i, 0)),
                  pl.BlockSpec((tm, LANES), lambda i, *_: (i, 0)),
                  pl.BlockSpec((1, 1, d), lambda i, *_: (i // tpb, 0, 0)),
                  pl.BlockSpec((1, d), lambda i, *_: (0, 0))],
        out_specs=pl.BlockSpec((tm, d), lambda i, *_: (i, 0)),
        scratch_shapes=[pltpu.VMEM((TOP_K, tm, d), F32), pltpu.SemaphoreType.DMA])
    return pl.pallas_call(
        functools.partial(_moe_combine_kernel, tm=tm, final=final), grid_spec=grid_spec,
        out_shape=jax.ShapeDtypeStruct((n, d), F32),
        compiler_params=_cparams(("arbitrary",)),
        name="moe_combine",
    )(dest, ys, x1, gate, gf, g_final)


def _route_tables(top_e, rank, counts, n_blocks):
    n_experts = counts.shape[0]
    padded = (counts + MOE_BLOCK - 1) // MOE_BLOCK * MOE_BLOCK
    pend = jnp.cumsum(padded)
    pstart = pend - padded
    onehot = top_e[..., None] == jnp.arange(n_experts, dtype=I32)
    dest = (jnp.sum(jnp.where(onehot, pstart, 0), axis=-1) + rank).reshape(-1).astype(I32)
    block_e = jnp.minimum(jnp.sum(pend[None, :] <= jnp.arange(n_blocks, dtype=I32)[:, None] * MOE_BLOCK, axis=1),
                          n_experts - 1).astype(I32)
    n_used = (pend[-1:] // MOE_BLOCK).astype(I32)
    pend0 = jnp.concatenate([jnp.zeros((1,), I32), pend.astype(I32)])
    return dest, pend0, block_e, n_used


def kernel(x, c, positions, rel_bias, norm_mix, w_ada, b_ada, w_in, q_norm, w_uq, kv_norm, w_ukv, w_out, norm_ffn, w_router, b_router, w1, b1, w2, b2, norm_final):
    bsz, seq, d = x.shape
    depth = w_ada.shape[0]
    n = bsz * seq
    n_experts = w_router.shape[-1]
    mod = _ada_mod(c, w_ada, b_ada)
    ctab, stab = _rope_tables(positions)
    row = lambda v: v.reshape(1, -1)
    x2 = x.reshape(n, d)
    for l in range(depth):
        sh_m, sc_m, g_m, sh_f, sc_f, g_f = [mod[l, :, i * d:(i + 1) * d].reshape(bsz, 1, d) for i in range(6)]
        wuq, wuqs = _pack_uq(w_uq[l])
        wukk, wukv = _pack_ukv(w_ukv[l])
        qa, ka, va, qi, ki, wi, qb, kb, vb = _proj(
            x2, row(norm_mix[l]), sc_m, sh_m, _pack_in_weights(w_in[l]), row(q_norm[l]), row(kv_norm[l]),
            wuq, wuqs, wukk, wukv, ctab, stab, seq)
        ya = _attn_a(qa, qi, wi, ka, va, ki, positions, rel_bias)
        yb = _attn_b(qb, kb, vb, positions)
        x1, h2, top_e, gate, rank, counts = _out_router(x2, ya, yb, w_out[l].astype(BF16), g_m, row(norm_ffn[l]),
                                                        sc_f, sh_f, w_router[l], b_router[l], seq)
        n_blocks = -(-n * TOP_K // MOE_BLOCK) + n_experts
        dest, pend0, block_e, n_used = _route_tables(top_e[:, :TOP_K], rank[:, :TOP_K], counts[0, :n_experts],
                                                     n_blocks)
        xs = _moe_dispatch(dest, pend0, h2, n_blocks * MOE_BLOCK)
        ys = _moe_experts(block_e + l * n_experts, n_used, xs, w1, b1, w2, b2)
        x2 = _moe_combine(dest, ys, x1, gate, g_f, row(norm_final), seq, final=(l == depth - 1))
    return x2.reshape(bsz, seq, d)
```

```python
import functools
import math

import numpy as np
import jax
import jax.numpy as jnp
from jax import lax
from jax.experimental import pallas as pl
from jax.experimental.pallas import tpu as pltpu

CHUNK = 64
EPS = 1e-6
A_HEADS = 8
A_HEAD_DIM = 64
IDX_HEADS = 8
IDX_DIM = 64
TOPK_MAX = 256
B_HEADS = 8
Q_LORA = 256
KV_LORA = 128
QK_NOPE = 64
QK_ROPE = 32
V_DIM = 64
ROPE_THETA = 10000.0
N_BUCKETS = 32
MAX_DISTANCE = 128
TOP_K = 4
SWIGLU_LIMIT = 7.0
SWIGLU_ALPHA = 1.702
MOE_BLOCK = 256

LANES = 128
SUBLANES = 8
VMEM_LIMIT = 56 * 1024 * 1024
INT_MIN = -2 ** 31
NEG_INF = float("-inf")
BIAS_ROWS = 32

F32 = jnp.float32
BF16 = jnp.bfloat16
I32 = jnp.int32

_T5_EXACT = (N_BUCKETS // 2) // 2
_T5_THRESH = tuple(
    int(math.ceil(_T5_EXACT * (MAX_DISTANCE / _T5_EXACT) ** (k / (N_BUCKETS // 2 - _T5_EXACT)) - 1e-9))
    for k in range(1, N_BUCKETS // 2 - _T5_EXACT))
_T5_FAR = _T5_THRESH[-1]


def _cparams(sem, vmem=VMEM_LIMIT, **kw):
    return pltpu.CompilerParams(dimension_semantics=sem, vmem_limit_bytes=vmem, **kw)


def _dot_t(a, b):
    return lax.dot_general(a, b, (((1,), (1,)), ((), ())), preferred_element_type=F32)


def _ada_kernel(c_ref, w_ref, b_ref, o_ref):
    c = c_ref[...]
    cond = c * jax.nn.sigmoid(c)
    o_ref[0] = jnp.dot(cond, w_ref[0], preferred_element_type=F32,
                       precision=lax.Precision.HIGHEST) + b_ref[0]


def _ada_mod(c, w_ada, b_ada, tn=512):
    depth, d, n6 = w_ada.shape
    b = c.shape[0]
    return pl.pallas_call(
        _ada_kernel,
        grid=(depth, n6 // tn),
        in_specs=[pl.BlockSpec((b, d), lambda l, j: (0, 0)),
                  pl.BlockSpec((1, d, tn), lambda l, j: (l, 0, j)),
                  pl.BlockSpec((1, 1, tn), lambda l, j: (l, 0, j))],
        out_specs=pl.BlockSpec((1, b, tn), lambda l, j: (l, 0, j)),
        out_shape=jax.ShapeDtypeStruct((depth, b, n6), F32),
        compiler_params=_cparams(("arbitrary", "arbitrary")),
        name="ada_mod",
    )(c, w_ada, b_ada.reshape(depth, 1, n6))


D_A = A_HEADS * A_HEAD_DIM
D_B = B_HEADS * V_DIM
HB = 128
_C_QA, _C_KA, _C_VA, _C_QI = 0, D_A, 2 * D_A, 3 * D_A
_C_KI = 4 * D_A
_C_WI = _C_KI + LANES
_C_CQ = _C_WI + LANES
_C_CKV = _C_CQ + Q_LORA
_C_KR = _C_CKV + KV_LORA
_C_KRS = _C_KR + LANES
_C_END = _C_KRS + LANES


def _pack_in_weights(w_in):
    d = w_in.shape[0]
    offs = np.cumsum((0, D_A, D_A, D_A, IDX_HEADS * IDX_DIM, IDX_DIM, IDX_HEADS, Q_LORA, KV_LORA, QK_ROPE))
    seg = [w_in[:, offs[i]:offs[i + 1]] for i in range(9)]
    q_a, k_a, v_a, q_i, k_i, w_i, c_q, c_kv, k_r = seg
    z = lambda n: jnp.zeros((d, n), w_in.dtype)
    half = QK_ROPE // 2
    k_rs = jnp.concatenate([k_r[:, half:], k_r[:, :half]], axis=1)
    cols = [q_a * (A_HEAD_DIM ** -0.5), k_a, v_a, q_i, k_i, k_i, w_i, z(LANES - IDX_HEADS), c_q, c_kv,
            z(QK_NOPE), k_r, z(HB - QK_NOPE - QK_ROPE), z(QK_NOPE), k_rs, z(HB - QK_NOPE - QK_ROPE)]
    return jnp.concatenate(cols, axis=1).astype(BF16)


def _pack_uq(w_uq):
    r = w_uq.shape[0]
    w = w_uq.reshape(r, B_HEADS, QK_NOPE + QK_ROPE)
    nope, rope = w[..., :QK_NOPE], w[..., QK_NOPE:]
    half = QK_ROPE // 2
    z = jnp.zeros((r, B_HEADS, HB - QK_NOPE - QK_ROPE), w.dtype)
    main = jnp.concatenate([nope, rope, z], axis=-1).reshape(r, B_HEADS * HB)
    swap = jnp.concatenate([jnp.zeros_like(nope), rope[..., half:], rope[..., :half], z], axis=-1)
    return main.astype(BF16), swap.reshape(r, B_HEADS * HB).astype(BF16)


def _pack_ukv(w_ukv):
    r = w_ukv.shape[0]
    w = w_ukv.reshape(r, B_HEADS, QK_NOPE + V_DIM)
    k = jnp.concatenate([w[..., :QK_NOPE], jnp.zeros((r, B_HEADS, HB - QK_NOPE), w.dtype)], axis=-1)
    v = w[..., QK_NOPE:]
    return k.reshape(r, B_HEADS * HB).astype(BF16), v.reshape(r, B_HEADS * V_DIM).astype(BF16)


def _rope_tables(positions):
    half = QK_ROPE // 2
    inv = ROPE_THETA ** (-jnp.arange(half, dtype=F32) / half)
    ang = positions.astype(F32).reshape(-1, 1) * inv[None, :]
    cos, sin = jnp.cos(ang), jnp.sin(ang)
    n = ang.shape[0]
    one = jnp.ones((n, QK_NOPE), F32)
    z = lambda k: jnp.zeros((n, k), F32)
    ctab = jnp.concatenate([one, cos, cos, z(HB - QK_NOPE - QK_ROPE)], axis=1)
    stab = jnp.concatenate([z(QK_NOPE), -sin, sin, z(HB - QK_NOPE - QK_ROPE)], axis=1)
    return ctab, stab


def _rms(x, g):
    return x * lax.rsqrt(jnp.mean(x * x, axis=-1, keepdims=True) + EPS) * g


def _proj_kernel(x_ref, g_ref, sc_ref, sh_ref, wp_ref, qn_ref, kvn_ref, wuq_ref, wuqs_ref, wukk_ref, wukv_ref,
                 ct_ref, st_ref,
                 qa_ref, ka_ref, va_ref, qi_ref, ki_ref, wi_ref, qb_ref, kb_ref, vb_ref):
    x = x_ref[...]
    h = _rms(x, g_ref[...]) * (1.0 + sc_ref[0]) + sh_ref[0]
    hb = h.astype(BF16)
    seg = lambda a, b: jnp.dot(hb, wp_ref[:, a:b], preferred_element_type=F32)
    qa_ref[...] = seg(_C_QA, _C_KA).astype(BF16)
    ka_ref[...] = seg(_C_KA, _C_VA).astype(BF16)
    va_ref[...] = seg(_C_VA, _C_QI).astype(BF16)
    qi_ref[...] = seg(_C_QI, _C_KI).astype(BF16)
    ki_ref[...] = seg(_C_KI, _C_WI).astype(BF16)
    wi_ref[...] = seg(_C_WI, _C_CQ)
    ct = ct_ref[...]
    st = st_ref[...]
    cq = _rms(seg(_C_CQ, _C_CKV), qn_ref[...]).astype(BF16)
    scale = (QK_NOPE + QK_ROPE) ** -0.5
    for hd in range(B_HEADS):
        sl = slice(hd * HB, (hd + 1) * HB)
        q = jnp.dot(cq, wuq_ref[:, sl], preferred_element_type=F32)
        qs = jnp.dot(cq, wuqs_ref[:, sl], preferred_element_type=F32)
        qb_ref[:, sl] = ((q * ct + qs * st) * scale).astype(BF16)
    ckv = _rms(seg(_C_CKV, _C_KR), kvn_ref[...]).astype(BF16)
    kr = seg(_C_KR, _C_KRS) * ct + seg(_C_KRS, _C_END) * st
    for hd in range(B_HEADS):
        sl = slice(hd * HB, (hd + 1) * HB)
        kb_ref[:, sl] = (jnp.dot(ckv, wukk_ref[:, sl], preferred_element_type=F32) + kr).astype(BF16)
    vb_ref[...] = jnp.dot(ckv, wukv_ref[...], preferred_element_type=F32).astype(BF16)


def _proj(x2, g, sc, sh, wp, qn, kvn, wuq, wuqs, wukk, wukv, ctab, stab, seq, tm=256):
    n, d = x2.shape
    tpb = seq // tm
    row = lambda w: pl.BlockSpec((tm, w), lambda i: (i, 0))
    full = lambda a: pl.BlockSpec(a.shape, lambda i: (0,) * a.ndim)
    per_b = pl.BlockSpec((1, 1, d), lambda i: (i // tpb, 0, 0))
    outs = [(D_A, BF16), (D_A, BF16), (D_A, BF16), (D_A, BF16), (LANES, BF16), (LANES, F32),
            (B_HEADS * HB, BF16), (B_HEADS * HB, BF16), (D_B, BF16)]
    return pl.pallas_call(
        _proj_kernel,
        grid=(n // tm,),
        in_specs=[row(d), full(g), per_b, per_b, full(wp), full(qn), full(kvn), full(wuq), full(wuqs),
                  full(wukk), full(wukv), row(HB), row(HB)],
        out_specs=[row(w) for w, _ in outs],
        out_shape=[jax.ShapeDtypeStruct((n, w), dt) for w, dt in outs],
        compiler_params=_cparams(("arbitrary",)),
        name="in_proj",
    )(x2, g, sc, sh, wp, qn, kvn, wuq, wuqs, wukk, wukv, ctab, stab)


def _pair_mask(x_pair, odd):
    lane = lax.broadcasted_iota(I32, x_pair.shape, 1)
    keep = (lane >= A_HEAD_DIM) if odd else (lane < A_HEAD_DIM)
    return jnp.where(keep, x_pair, jnp.zeros_like(x_pair))


def _flash_step(hd, s, v_pair, m_ref, l_ref, acc_ref):
    m_prev = m_ref[hd]
    m_new = jnp.maximum(m_prev, jnp.max(s, axis=1, keepdims=True))
    m_safe = jnp.where(m_new == NEG_INF, 0.0, m_new)
    alpha = jnp.exp(m_prev - m_safe)
    p = jnp.exp(s - jnp.concatenate([m_safe] * (s.shape[1] // LANES), axis=1))
    l_ref[hd] = alpha * l_ref[hd] + jnp.sum(p, axis=1, keepdims=True)
    acc_ref[hd] = alpha * acc_ref[hd] + jnp.dot(p.astype(BF16), v_pair, preferred_element_type=F32)
    m_ref[hd] = m_new


def _flash_init(m_ref, l_ref, acc_ref):
    m_ref[...] = jnp.full(m_ref.shape, NEG_INF, F32)
    l_ref[...] = jnp.zeros(l_ref.shape, F32)
    acc_ref[...] = jnp.zeros(acc_ref.shape, F32)


def _flash_finish(y_ref, l_ref, acc_ref, n_heads):
    lane = lax.broadcasted_iota(I32, acc_ref.shape[1:], 1)
    for j in range(n_heads // 2):
        even = acc_ref[2 * j] / l_ref[2 * j]
        odd = acc_ref[2 * j + 1] / l_ref[2 * j + 1]
        y_ref[:, j * LANES:(j + 1) * LANES] = jnp.where(lane < V_DIM, even, odd).astype(y_ref.dtype)


def _t5_bias_heads(rel, bias_ref, lo, hi):
    nb = N_BUCKETS // 2
    n = jnp.abs(rel)
    bucket = jnp.minimum(n, _T5_EXACT)
    for t in _T5_THRESH:
        bucket = bucket + (n >= t).astype(I32)
    if hi >= nb:
        bucket = bucket + (nb if lo >= nb else jnp.where(rel > 0, nb, 0))
    outs = [jnp.full(rel.shape, bias_ref[lo * A_HEADS + hd], F32) for hd in range(A_HEADS)]
    for j in range(lo + 1, hi + 1):
        hit = bucket == j
        outs = [jnp.where(hit, bias_ref[j * A_HEADS + hd], outs[hd]) for hd in range(A_HEADS)]
    return outs


def _attn_a_kernel(nk_ref, qlo_ref, qhi_ref, klo_ref, khi_ref, rlo_ref, rhi_ref,
                   bias_ref, qa_ref, qi_ref, wi_ref, ka_ref, va_ref, ki_ref, pq_ref, pk_ref,
                   y_ref,
                   key_ref, keyt_ref, am_ref, ex_ref, j_ref, wb_ref, m_ref, l_ref, acc_ref,
                   *, topk, tq, tk, nq, nkt):
    b = pl.program_id(0)
    qi_idx = pl.program_id(1)
    nk = nk_ref[b * nq + qi_idx]
    pq = pq_ref[...]
    qchunk = jnp.right_shift(pq, int(math.log2(CHUNK)))
    lane_tk = lax.broadcasted_iota(I32, (tq, tk), 1)

    wide = lambda v: jnp.concatenate([v] * (tk // LANES), axis=1)
    for hd in range(IDX_HEADS):
        wb_ref[hd] = jnp.broadcast_to(wi_ref[:, hd:hd + 1], (tq, LANES))

    def score_body(kt, carry):
        ks = ki_ref[pl.ds(pl.multiple_of(kt * tk, tk), tk), :]
        sc = jnp.zeros((tq, tk), F32)
        for hd in range(IDX_HEADS):
            pair = qi_ref[:, (hd // 2) * LANES:(hd // 2 + 1) * LANES]
            r = _dot_t(_pair_mask(pair, hd % 2), ks)
            sc = sc + wide(wb_ref[hd]) * jnp.maximum(r, 0.0)
        bits = pltpu.bitcast(sc, I32)
        key = bits ^ (jnp.right_shift(bits, 31) & jnp.int32(0x7FFFFFFF))
        kchunk = jnp.right_shift(pk_ref[kt], int(math.log2(CHUNK)))
        key = jnp.where(kchunk <= qchunk, key, jnp.int32(INT_MIN))
        key_ref[kt] = key
        keyt_ref[kt] = key.T
        return carry
    lax.fori_loop(0, nk, score_body, 0)

    groups = tk // SUBLANES
    sub_idx = (lax.broadcasted_iota(I32, (groups, SUBLANES, tq), 0) * SUBLANES
               + lax.broadcasted_iota(I32, (groups, SUBLANES, tq), 1))

    def count(pred):
        def body(kt, acc):
            k3 = keyt_ref[kt].reshape(groups, SUBLANES, tq)
            return acc + jnp.sum(jnp.where(pred(k3, kt * tk), 1, 0), axis=0)
        acc = lax.fori_loop(0, nk, body, jnp.zeros((SUBLANES, tq), I32))
        return jnp.broadcast_to(jnp.sum(acc, axis=0, keepdims=True), (SUBLANES, tq))

    def bit_body(i, t):
        cand = t + jnp.left_shift(jnp.int32(1), 31 - i)
        c = count(lambda k, k0: k >= cand)
        return jnp.where(c >= topk, cand, t)
    thr_t = lax.fori_loop(0, 32, bit_body, jnp.full((SUBLANES, tq), INT_MIN, I32))
    need = topk - count(lambda k, k0: k > thr_t)
    n_eq = count(lambda k, k0: k == thr_t)
    real = thr_t > jnp.int32(INT_MIN)
    j_ref[...] = jnp.where(real, jnp.int32(nkt * tk), jnp.int32(-1))
    tie = jnp.max(jnp.where(real & (n_eq > need), 1, 0))

    @pl.when(tie > 0)
    def _():
        def idx_body(i, p):
            cand = p + jnp.left_shift(jnp.int32(1), (nkt * tk).bit_length() - 2 - i)
            c = count(lambda k, k0: (k == thr_t) & (sub_idx + k0 < cand))
            return jnp.where(c < need, cand, p)
        p = lax.fori_loop(0, (nkt * tk).bit_length() - 1, idx_body, jnp.zeros((SUBLANES, tq), I32))
        j_ref[...] = jnp.where(real & (n_eq > need), p, j_ref[...])

    to_rows = lambda v: jnp.broadcast_to(v[:1], (LANES, tq)).T
    thr_w = wide(to_rows(thr_t))
    jsel_w = wide(to_rows(j_ref[...]))

    def mask_body(kt, carry):
        k = key_ref[kt]
        sel = (k > thr_w) | ((k == thr_w) & (lane_tk + kt * tk <= jsel_w))
        am_ref[kt] = jnp.where(sel, 0.0, NEG_INF)
        return carry
    lax.fori_loop(0, nk, mask_body, 0)

    _flash_init(m_ref, l_ref, acc_ref)
    q_lo = qlo_ref[b * nq + qi_idx]
    q_hi = qhi_ref[b * nq + qi_idx]

    shift = int(math.log2(CHUNK))
    chunk0 = b * (nkt * tk // LANES)
    block0 = b * (nq * tq // BIAS_ROWS) + qi_idx * (tq // BIAS_ROWS)

    def bias_chunk(kt, c):
        cs = slice(c * LANES, (c + 1) * LANES)
        k_lo = klo_ref[chunk0 + kt * (tk // LANES) + c]
        k_hi = khi_ref[chunk0 + kt * (tk // LANES) + c]
        far_past = k_hi - q_lo <= -_T5_FAR
        far_future = k_lo - q_hi >= _T5_FAR

        @pl.when(far_past)
        def _():
            for hd in range(A_HEADS):
                ex_ref[hd, :, cs] = am_ref[kt, :, cs]

        @pl.when(far_future)
        def _():
            for hd in range(A_HEADS):
                ex_ref[hd, :, cs] = am_ref[kt, :, cs] + bias_ref[(N_BUCKETS - 1) * A_HEADS + hd]

        @pl.when(jnp.logical_not(far_past | far_future))
        def _():
            pkc = pk_ref[kt][:, cs]

            def rows_body(r, carry):
                rows = pl.ds(pl.multiple_of(r * BIAS_ROWS, BIAS_ROWS), BIAS_ROWS)
                r_lo = rlo_ref[block0 + r]
                r_hi = rhi_ref[block0 + r]
                hidden = jnp.right_shift(k_lo, shift) > jnp.right_shift(r_hi, shift)
                plain = hidden | (k_hi - r_lo <= -_T5_FAR)
                behind = (k_hi <= r_lo) & jnp.logical_not(plain)
                ahead = (k_lo > r_hi) & jnp.logical_not(plain)
                mixed = jnp.logical_not(plain | behind | ahead)

                def emit(lo, hi):
                    am = am_ref[kt, rows, cs]
                    for hd, bias in enumerate(_t5_bias_heads(pkc - pq_ref[rows, :], bias_ref, lo, hi)):
                        ex_ref[hd, rows, cs] = am + bias

                @pl.when(plain)
                def _():
                    for hd in range(A_HEADS):
                        ex_ref[hd, rows, cs] = am_ref[kt, rows, cs]
                pl.when(behind)(lambda: emit(0, N_BUCKETS // 2 - 1))
                pl.when(ahead)(lambda: emit(N_BUCKETS // 2, N_BUCKETS - 1))
                pl.when(mixed)(lambda: emit(0, N_BUCKETS - 1))
                return carry
            lax.fori_loop(0, tq // BIAS_ROWS, rows_body, 0)

    def flash_heads(kt, extra):
        start = pl.multiple_of(kt * tk, tk)
        for hd in range(A_HEADS):
            pr = slice((hd // 2) * LANES, (hd // 2 + 1) * LANES)
            qm = _pair_mask(qa_ref[:, pr], hd % 2)
            s = _dot_t(qm, ka_ref[pl.ds(start, tk), pr]) + extra(hd)
            _flash_step(hd, s, va_ref[pl.ds(start, tk), pr], m_ref, l_ref, acc_ref)

    def attn_body(kt, carry):
        tile_hi = khi_ref[chunk0 + kt * (tk // LANES)]
        for c in range(1, tk // LANES):
            tile_hi = jnp.maximum(tile_hi, khi_ref[chunk0 + kt * (tk // LANES) + c])
        far_tile = tile_hi - q_lo <= -_T5_FAR

        @pl.when(far_tile)
        def _():
            flash_heads(kt, lambda hd: am_ref[kt])

        @pl.when(jnp.logical_not(far_tile))
        def _():
            for c in range(tk // LANES):
                bias_chunk(kt, c)
            flash_heads(kt, lambda hd: ex_ref[hd])
        return carry
    lax.fori_loop(0, nk, attn_body, 0)
    _flash_finish(y_ref, l_ref, acc_ref, A_HEADS)


def _tile_tables(positions, tq, tk):
    b, s = positions.shape
    pq = positions.reshape(b, s // tq, tq)
    pk = positions.reshape(b, s // tk, tk)
    pc = positions.reshape(b, s // LANES, LANES)
    pr = positions.reshape(b, s // BIAS_ROWS, BIAS_ROWS)
    q_lo, q_hi = pq.min(-1), pq.max(-1)
    vis = (pk.min(-1) // CHUNK)[:, None, :] <= (q_hi // CHUNK)[:, :, None]
    last = jnp.max(jnp.where(vis, jnp.arange(s // tk, dtype=I32)[None, None, :] + 1, 0), axis=-1)
    flat = lambda a: a.reshape(-1).astype(I32)
    return (flat(last), flat(q_lo), flat(q_hi), flat(pc.min(-1)), flat(pc.max(-1)),
            flat(pr.min(-1)), flat(pr.max(-1)))


def _attn_a(qa, qi, wi, ka, va, ki, positions, rel_bias, tq=512, tk=512):
    b, s = positions.shape
    n = b * s
    nq, nkt = s // tq, s // tk
    topk = min(TOPK_MAX, s // 4)
    tables = _tile_tables(positions, tq, tk)
    pos_col = positions.reshape(n, 1)
    pos_row = positions.reshape(b, nkt, 1, tk)
    qrow = lambda w: pl.BlockSpec((tq, w), lambda bi, i, *_: (bi * nq + i, 0))
    kfull = lambda w: pl.BlockSpec((s, w), lambda bi, i, *_: (bi, 0))
    kern = functools.partial(_attn_a_kernel, topk=topk, tq=tq, tk=tk, nq=nq, nkt=nkt)
    grid_spec = pltpu.PrefetchScalarGridSpec(
        num_scalar_prefetch=len(tables),
        grid=(b, nq),
        in_specs=[pl.BlockSpec(memory_space=pltpu.SMEM),
                  qrow(D_A), qrow(D_A), qrow(LANES), kfull(D_A), kfull(D_A), kfull(LANES),
                  pl.BlockSpec((tq, 1), lambda bi, i, *_: (bi * nq + i, 0)),
                  pl.BlockSpec((None, nkt, 1, tk), lambda bi, i, *_: (bi, 0, 0, 0))],
        out_specs=qrow(D_A),
        scratch_shapes=[pltpu.VMEM((nkt, tq, tk), I32), pltpu.VMEM((nkt, tk, tq), I32),
                        pltpu.VMEM((nkt, tq, tk), F32),
                        pltpu.VMEM((A_HEADS, tq, tk), F32), pltpu.VMEM((SUBLANES, tq), I32),
                        pltpu.VMEM((IDX_HEADS, tq, LANES), F32),
                        pltpu.VMEM((A_HEADS, tq, LANES), F32), pltpu.VMEM((A_HEADS, tq, LANES), F32),
                        pltpu.VMEM((A_HEADS, tq, LANES), F32)])
    return pl.pallas_call(
        kern, grid_spec=grid_spec,
        out_shape=jax.ShapeDtypeStruct((n, D_A), BF16),
        compiler_params=_cparams(("arbitrary", "arbitrary")),
        name="attn_indexer",
    )(*tables, (rel_bias - rel_bias[N_BUCKETS // 2 - 1]).reshape(-1), qa, qi, wi, ka, va, ki, pos_col, pos_row)


def _attn_b_kernel(nk_ref, qb_ref, kb_ref, vb_ref, pq_ref, pk_ref, y_ref, m_ref, l_ref, acc_ref, *, tq, tk, nq):
    b = pl.program_id(0)
    nk = nk_ref[b * nq + pl.program_id(1)]
    qchunk = jnp.right_shift(pq_ref[...], int(math.log2(CHUNK)))
    _flash_init(m_ref, l_ref, acc_ref)

    def body(kt, carry):
        start = pl.multiple_of(kt * tk, tk)
        kchunk = jnp.right_shift(pk_ref[kt], int(math.log2(CHUNK)))
        am = jnp.where(kchunk <= qchunk, 0.0, NEG_INF)
        for hd in range(B_HEADS):
            hs = slice(hd * HB, (hd + 1) * HB)
            pr = slice((hd // 2) * LANES, (hd // 2 + 1) * LANES)
            s = _dot_t(qb_ref[:, hs], kb_ref[pl.ds(start, tk), hs]) + am
            _flash_step(hd, s, vb_ref[pl.ds(start, tk), pr], m_ref, l_ref, acc_ref)
        return carry
    lax.fori_loop(0, nk, body, 0)
    _flash_finish(y_ref, l_ref, acc_ref, B_HEADS)


def _attn_b(qb, kb, vb, positions, tq=512, tk=512):
    b, s = positions.shape
    n = b * s
    nq, nkt = s // tq, s // tk
    nk = _tile_tables(positions, tq, tk)[0]
    qrow = lambda w: pl.BlockSpec((tq, w), lambda bi, i, *_: (bi * nq + i, 0))
    kfull = lambda w: pl.BlockSpec((s, w), lambda bi, i, *_: (bi, 0))
    grid_spec = pltpu.PrefetchScalarGridSpec(
        num_scalar_prefetch=1,
        grid=(b, nq),
        in_specs=[qrow(B_HEADS * HB), kfull(B_HEADS * HB), kfull(D_B),
                  pl.BlockSpec((tq, 1), lambda bi, i, *_: (bi * nq + i, 0)),
                  pl.BlockSpec((None, nkt, 1, tk), lambda bi, i, *_: (bi, 0, 0, 0))],
        out_specs=qrow(D_B),
        scratch_shapes=[pltpu.VMEM((B_HEADS, tq, LANES), F32)] * 3)
    return pl.pallas_call(
        functools.partial(_attn_b_kernel, tq=tq, tk=tk, nq=nq), grid_spec=grid_spec,
        out_shape=jax.ShapeDtypeStruct((n, D_B), BF16),
        compiler_params=_cparams(("arbitrary", "arbitrary")),
        name="attn_latent",
    )(nk, qb, kb, vb, positions.reshape(n, 1), positions.reshape(b, nkt, 1, tk))


def _out_router_kernel(x_ref, ya_ref, yb_ref, wo_ref, gm_ref, g_ref, sc_ref, sh_ref, wr_ref, br_ref,
                       x1_ref, h2_ref, e_ref, gate_ref, rank_ref, cnt_ref, run_ref, *, n_experts):
    @pl.when(pl.program_id(0) == 0)
    def _():
        run_ref[...] = jnp.zeros(run_ref.shape, F32)

    mix = (jnp.dot(ya_ref[...], wo_ref[:D_A, :], preferred_element_type=F32)
           + jnp.dot(yb_ref[...], wo_ref[D_A:, :], preferred_element_type=F32))
    x1 = x_ref[...] + gm_ref[0] * mix
    x1_ref[...] = x1
    h2 = _rms(x1, g_ref[...]) * (1.0 + sc_ref[0]) + sh_ref[0]
    h2_ref[...] = h2
    logits = jnp.dot(h2, wr_ref[...], preferred_element_type=F32, precision=lax.Precision.HIGHEST) + br_ref[...]
    lane = lax.broadcasted_iota(I32, logits.shape, 1)
    cur = jnp.where(lane < n_experts, logits, NEG_INF)
    e_out = jnp.zeros(logits.shape, I32)
    g_out = jnp.zeros(logits.shape, F32)
    top = None
    picks = []
    for k in range(TOP_K):
        m = jnp.max(cur, axis=1, keepdims=True)
        idx = jnp.min(jnp.where(cur == m, lane, LANES), axis=1, keepdims=True)
        top = m if top is None else top
        e_out = jnp.where(lane == k, idx, e_out)
        g_out = jnp.where(lane == k, jnp.exp(m - top), g_out)
        picks.append(lane == idx)
        cur = jnp.where(picks[-1], NEG_INF, cur)
    e_ref[...] = e_out
    gate_ref[...] = g_out / jnp.sum(g_out, axis=1, keepdims=True)
    tm = logits.shape[0]
    chosen = jnp.where(picks[0] | picks[1] | picks[2] | picks[3], 1.0, 0.0)
    earlier = (lax.broadcasted_iota(I32, (tm, tm), 0) > lax.broadcasted_iota(I32, (tm, tm), 1))
    before = jnp.dot(jnp.where(earlier, 1.0, 0.0).astype(BF16), chosen.astype(BF16),
                     preferred_element_type=F32) + run_ref[0:1, :]
    r_out = jnp.zeros(logits.shape, I32)
    for k in range(TOP_K):
        rk = jnp.sum(jnp.where(picks[k], before, 0.0), axis=1, keepdims=True)
        r_out = jnp.where(lane == k, rk.astype(I32), r_out)
    rank_ref[...] = r_out
    run_ref[...] = run_ref[...] + jnp.sum(chosen, axis=0, keepdims=True)
    cnt_ref[...] = run_ref[...].astype(I32)


def _out_router(x2, ya, yb, wo, gm, g, sc, sh, wr, br, seq, tm=512):
    n, d = x2.shape
    tpb = seq // tm
    n_experts = wr.shape[1]
    wr_p = jnp.pad(wr, ((0, 0), (0, LANES - n_experts)))
    br_p = jnp.pad(br, (0, LANES - n_experts)).reshape(1, LANES)
    row = lambda w: pl.BlockSpec((tm, w), lambda i: (i, 0))
    full = lambda a: pl.BlockSpec(a.shape, lambda i: (0,) * a.ndim)
    per_b = pl.BlockSpec((1, 1, d), lambda i: (i // tpb, 0, 0))
    return pl.pallas_call(
        functools.partial(_out_router_kernel, n_experts=n_experts),
        grid=(n // tm,),
        in_specs=[row(d), row(D_A), row(D_B), full(wo), per_b, full(g), per_b, per_b, full(wr_p), full(br_p)],
        out_specs=[row(d), row(d), row(LANES), row(LANES), row(LANES),
                   pl.BlockSpec((SUBLANES, LANES), lambda i: (0, 0))],
        out_shape=[jax.ShapeDtypeStruct((n, d), F32), jax.ShapeDtypeStruct((n, d), F32),
                   jax.ShapeDtypeStruct((n, LANES), I32), jax.ShapeDtypeStruct((n, LANES), F32),
                   jax.ShapeDtypeStruct((n, LANES), I32), jax.ShapeDtypeStruct((SUBLANES, LANES), I32)],
        scratch_shapes=[pltpu.VMEM((SUBLANES, LANES), F32)],
        compiler_params=_cparams(("arbitrary",)),
        name="out_proj_router",
    )(x2, ya, yb, wo, gm, g, sc, sh, wr_p, br_p)


def _row_copy(src_hbm, row, dst_ref, slot, sem):
    return pltpu.make_async_copy(src_hbm.at[pl.ds(row, 1), :], dst_ref.at[pl.ds(slot, 1), :], sem)


def _moe_dispatch_kernel(dest_ref, pend_ref, h_ref, xs_hbm, zero_ref, zsem, sem, *, tm, n_experts):
    i = pl.program_id(0)

    @pl.when(i == 0)
    def _():
        zero_ref[...] = jnp.zeros(zero_ref.shape, zero_ref.dtype)

        def tail(e):
            start = pl.multiple_of(pend_ref[e + 1] - MOE_BLOCK, MOE_BLOCK)
            return pltpu.make_async_copy(zero_ref, xs_hbm.at[pl.ds(start, MOE_BLOCK), :], zsem)

        def has_slots(e):
            return pend_ref[e + 1] > pend_ref[e]

        def unused(j):
            start = pl.multiple_of(pend_ref[n_experts] + j * MOE_BLOCK, MOE_BLOCK)
            return pltpu.make_async_copy(zero_ref, xs_hbm.at[pl.ds(start, MOE_BLOCK), :], zsem)

        def is_unused(j):
            return pend_ref[n_experts] + j * MOE_BLOCK < xs_hbm.shape[0]
        for e in range(n_experts):
            pl.when(has_slots(e))(lambda e=e: tail(e).start())
        for j in range(n_experts):
            pl.when(is_unused(j))(lambda j=j: unused(j).start())
        for e in range(n_experts):
            pl.when(has_slots(e))(lambda e=e: tail(e).wait())
        for j in range(n_experts):
            pl.when(is_unused(j))(lambda j=j: unused(j).wait())

    base = i * tm * TOP_K

    def issue(t, carry):
        for k in range(TOP_K):
            pltpu.make_async_copy(h_ref.at[pl.ds(t, 1), :],
                                  xs_hbm.at[pl.ds(dest_ref[base + t * TOP_K + k], 1), :], sem).start(priority=k % 2)
        return carry
    lax.fori_loop(0, tm, issue, 0, unroll=2)
    for k in range(TOP_K):
        pltpu.make_async_copy(h_ref, xs_hbm.at[pl.ds(0, tm), :], sem).wait()


def _moe_dispatch(dest, pend0, h2, cap, tm=512):
    n, d = h2.shape
    n_experts = pend0.shape[0] - 1
    grid_spec = pltpu.PrefetchScalarGridSpec(
        num_scalar_prefetch=2, grid=(n // tm,),
        in_specs=[pl.BlockSpec((tm, d), lambda i, *_: (i, 0))],
        out_specs=pl.BlockSpec(memory_space=pl.ANY),
        scratch_shapes=[pltpu.VMEM((MOE_BLOCK, d), h2.dtype), pltpu.SemaphoreType.DMA, pltpu.SemaphoreType.DMA])
    return pl.pallas_call(
        functools.partial(_moe_dispatch_kernel, tm=tm, n_experts=n_experts), grid_spec=grid_spec,
        out_shape=jax.ShapeDtypeStruct((cap, d), h2.dtype),
        compiler_params=_cparams(("arbitrary",)),
        name="moe_dispatch",
    )(dest, pend0, h2)


def _moe_expert_kernel(be_ref, nu_ref, xs_ref, w1_ref, b1_ref, w2_ref, b2_ref, y_ref, w1b_ref, w2b_ref, *, d_exp):
    i = pl.program_id(0)
    live = i < nu_ref[0]
    changed = jnp.logical_or(i == 0, be_ref[i] != be_ref[jnp.maximum(i - 1, 0)])

    @pl.when(live & changed)
    def _():
        w1b_ref[...] = w1_ref[0].astype(BF16)
        w2b_ref[...] = w2_ref[0].astype(BF16)

    @pl.when(live)
    def _():
        hid = jnp.dot(xs_ref[...].astype(BF16), w1b_ref[...], preferred_element_type=F32) + b1_ref[0]
        glu = jnp.minimum(hid[:, :d_exp], SWIGLU_LIMIT)
        lin = jnp.clip(hid[:, d_exp:], -SWIGLU_LIMIT, SWIGLU_LIMIT)
        act = glu * jax.nn.sigmoid(SWIGLU_ALPHA * glu) * (lin + 1.0)
        y_ref[...] = jnp.dot(act.astype(BF16), w2b_ref[...], preferred_element_type=F32) + b2_ref[0]

    @pl.when(jnp.logical_not(live))
    def _():
        y_ref[...] = jnp.zeros(y_ref.shape, y_ref.dtype)


def _moe_experts(block_e, n_used, xs, w1, b1, w2, b2):
    cap, d = xs.shape
    d2 = w1.shape[-1]
    d_exp = d2 // 2
    n_exp = w1.shape[0] * w1.shape[1]
    w1 = w1.reshape(n_exp, d, d2)
    w2 = w2.reshape(n_exp, d_exp, d)
    grid_spec = pltpu.PrefetchScalarGridSpec(
        num_scalar_prefetch=2, grid=(cap // MOE_BLOCK,),
        in_specs=[pl.BlockSpec((MOE_BLOCK, d), lambda i, be, nu: (jnp.minimum(i, nu[0] - 1), 0)),
                  pl.BlockSpec((1, d, d2), lambda i, be, nu: (be[i], 0, 0)),
                  pl.BlockSpec((1, 1, d2), lambda i, be, nu: (be[i], 0, 0)),
                  pl.BlockSpec((1, d_exp, d), lambda i, be, nu: (be[i], 0, 0)),
                  pl.BlockSpec((1, 1, d), lambda i, be, nu: (be[i], 0, 0))],
        out_specs=pl.BlockSpec((MOE_BLOCK, d), lambda i, be, nu: (i, 0)),
        scratch_shapes=[pltpu.VMEM((d, d2), BF16), pltpu.VMEM((d_exp, d), BF16)])
    return pl.pallas_call(
        functools.partial(_moe_expert_kernel, d_exp=d_exp), grid_spec=grid_spec,
        out_shape=jax.ShapeDtypeStruct((cap, d), F32),
        compiler_params=_cparams(("arbitrary",)),
        name="moe_experts",
    )(block_e, n_used, xs, w1, b1.reshape(n_exp, 1, d2), w2, b2.reshape(n_exp, 1, d))


def _moe_combine_kernel(dest_ref, ys_hbm, x1_ref, gate_ref, gf_ref, gfin_ref, o_ref, rows_ref, sem, *, tm, final):
    base = pl.program_id(0) * tm * TOP_K

    def issue(t, carry):
        for k in range(TOP_K):
            _row_copy(ys_hbm, dest_ref[base + t * TOP_K + k], rows_ref.at[k], t, sem).start(priority=k % 2)
        return carry
    lax.fori_loop(0, tm, issue, 0, unroll=2)
    for k in range(TOP_K):
        pltpu.make_async_copy(ys_hbm.at[pl.ds(0, tm), :], rows_ref.at[k], sem).wait()
    moe = gate_ref[:, 0:1] * rows_ref[0]
    for k in range(1, TOP_K):
        moe = moe + gate_ref[:, k:k + 1] * rows_ref[k]
    out = x1_ref[...] + gf_ref[0] * moe
    if final:
        out = _rms(out, gfin_ref[...])
    o_ref[...] = out


def _moe_combine(dest, ys, x1, gate, gf, g_final, seq, final, tm=256):
    n, d = x1.shape
    tpb = seq // tm
    grid_spec = pltpu.PrefetchScalarGridSpec(
        num_scalar_prefetch=1, grid=(n // tm,),
        in_specs=[pl.BlockSpec(memory_space=pl.ANY),
                  pl.BlockSpec((tm, d), lambda i, *_: (i, 0)),
                  pl.BlockSpec((tm, LANES), lambda i, *_: (i, 0)),
                  pl.BlockSpec((1, 1, d), lambda i, *_: (i // tpb, 0, 0)),
                  pl.BlockSpec((1, d), lambda i, *_: (0, 0))],
        out_specs=pl.BlockSpec((tm, d), lambda i, *_: (i, 0)),
        scratch_shapes=[pltpu.VMEM((TOP_K, tm, d), F32), pltpu.SemaphoreType.DMA])
    return pl.pallas_call(
        functools.partial(_moe_combine_kernel, tm=tm, final=final), grid_spec=grid_spec,
        out_shape=jax.ShapeDtypeStruct((n, d), F32),
        compiler_params=_cparams(("arbitrary",)),
        name="moe_combine",
    )(dest, ys, x1, gate, gf, g_final)


def _route_tables(top_e, rank, counts, n_blocks):
    n_experts = counts.shape[0]
    padded = (counts + MOE_BLOCK - 1) // MOE_BLOCK * MOE_BLOCK
    pend = jnp.cumsum(padded)
    pstart = pend - padded
    onehot = top_e[..., None] == jnp.arange(n_experts, dtype=I32)
    dest = (jnp.sum(jnp.where(onehot, pstart, 0), axis=-1) + rank).reshape(-1).astype(I32)
    block_e = jnp.minimum(jnp.sum(pend[None, :] <= jnp.arange(n_blocks, dtype=I32)[:, None] * MOE_BLOCK, axis=1),
                          n_experts - 1).astype(I32)
    n_used = (pend[-1:] // MOE_BLOCK).astype(I32)
    pend0 = jnp.concatenate([jnp.zeros((1,), I32), pend.astype(I32)])
    return dest, pend0, block_e, n_used


def kernel(x, c, positions, rel_bias, norm_mix, w_ada, b_ada, w_in, q_norm, w_uq, kv_norm, w_ukv, w_out, norm_ffn, w_router, b_router, w1, b1, w2, b2, norm_final):
    bsz, seq, d = x.shape
    depth = w_ada.shape[0]
    n = bsz * seq
    n_experts = w_router.shape[-1]
    mod = _ada_mod(c, w_ada, b_ada)
    ctab, stab = _rope_tables(positions)
    row = lambda v: v.reshape(1, -1)
    x2 = x.reshape(n, d)
    for l in range(depth):
        sh_m, sc_m, g_m, sh_f, sc_f, g_f = [mod[l, :, i * d:(i + 1) * d].reshape(bsz, 1, d) for i in range(6)]
        wuq, wuqs = _pack_uq(w_uq[l])
        wukk, wukv = _pack_ukv(w_ukv[l])
        qa, ka, va, qi, ki, wi, qb, kb, vb = _proj(
            x2, row(norm_mix[l]), sc_m, sh_m, _pack_in_weights(w_in[l]), row(q_norm[l]), row(kv_norm[l]),
            wuq, wuqs, wukk, wukv, ctab, stab, seq)
        ya = _attn_a(qa, qi, wi, ka, va, ki, positions, rel_bias)
        yb = _attn_b(qb, kb, vb, positions)
        x1, h2, top_e, gate, rank, counts = _out_router(x2, ya, yb, w_out[l].astype(BF16), g_m, row(norm_ffn[l]),
                                                        sc_f, sh_f, w_router[l], b_router[l], seq)
        n_blocks = -(-n * TOP_K // MOE_BLOCK) + n_experts
        dest, pend0, block_e, n_used = _route_tables(top_e[:, :TOP_K], rank[:, :TOP_K], counts[0, :n_experts],
                                                     n_blocks)
        xs = _moe_dispatch(dest, pend0, h2, n_blocks * MOE_BLOCK)
        ys = _moe_experts(block_e + l * n_experts, n_used, xs, w1, b1, w2, b2)
        x2 = _moe_combine(dest, ys, x1, gate, g_f, row(norm_final), seq, final=(l == depth - 1))
    return x2.reshape(bsz, seq, d)
```
